```python
import math
import jax, jax.numpy as jnp
from jax import lax
import numpy as np


D_MODEL = 1024
BATCH = 2
SEQ = 8192
DEPTH = 2

CHUNK = 64
N_META = 16
Q_BLOCK = 128

SB_HEADS = 8
SB_HEAD_DIM = 64
SB_WIDTH = SB_HEADS * SB_HEAD_DIM

RWKV_HEADS = 8
RWKV_HEAD_DIM = 64
RWKV_WIDTH = RWKV_HEADS * RWKV_HEAD_DIM
W_LORA = 64
A_LORA = 64
G_LORA = 128
RWKV_SHIFT_WIDTH = 3 * RWKV_WIDTH + W_LORA + A_LORA + G_LORA
DECAY_SCALE = math.exp(-0.5)
LNX_EPS = 64e-5

C_IN = 3 * SB_WIDTH + RWKV_SHIFT_WIDTH + 2 * D_MODEL
SPLITS = [int(s) for s in np.cumsum([SB_WIDTH, SB_WIDTH, SB_WIDTH, RWKV_SHIFT_WIDTH, D_MODEL])]
RWKV_SPLITS = [int(s) for s in np.cumsum([RWKV_WIDTH, RWKV_WIDTH, RWKV_WIDTH, W_LORA, A_LORA])]

N_EXPERTS = 16
N_GROUPS = 4
EXPERTS_PER_GROUP = N_EXPERTS // N_GROUPS
TOP_K = 2
D_EXPERT = 512

ALPHA = (2 * DEPTH) ** 0.25
BETA_INIT = (8 * DEPTH) ** -0.25
LN_EPS = 1e-5

kernel_name = "hybrid_stickbreak_rwkv7_moe_encoder"


def layer_norm(x, g, b):
    xf = x.astype(jnp.float32)
    mu = jnp.mean(xf, axis=-1, keepdims=True)
    var = jnp.mean(jnp.square(xf - mu), axis=-1, keepdims=True)
    return ((xf - mu) * lax.rsqrt(var + LN_EPS) * g + b).astype(x.dtype)


def token_shift(p):
    return jnp.pad(p[:, :-1], ((0, 0), (1, 0), (0, 0)))


def stick_breaking_attention(q, k, v):
    B, H, L, N = q.shape
    nblk = -(-L // Q_BLOCK)
    Lp = nblk * Q_BLOCK
    pad = ((0, 0), (0, 0), (0, Lp - L), (0, 0))
    q, k, v = jnp.pad(q, pad), jnp.pad(k, pad), jnp.pad(v, pad)
    scale = N ** -0.5
    key_pos = jnp.arange(Lp)

    def block(i):
        q_blk = lax.dynamic_slice_in_dim(q, i * Q_BLOCK, Q_BLOCK, axis=2)
        z = jnp.einsum('bhqn,bhkn->bhqk', q_blk, k) * scale
        q_pos = i * Q_BLOCK + jnp.arange(Q_BLOCK)
        mask = key_pos[None, :] < q_pos[:, None]
        u = jnp.where(mask, jax.nn.log_sigmoid(-z), 0.0)
        rest = lax.cumsum(u, axis=3, reverse=True) - u
        A = jnp.where(mask, jnp.exp(jax.nn.log_sigmoid(z) + rest), 0.0)
        return jnp.einsum('bhqk,bhkn->bhqn', A, v)

    out = lax.map(block, jnp.arange(nblk))
    out = jnp.moveaxis(out, 0, 2).reshape(B, H, Lp, N)
    return out[:, :, :L]


def rwkv7_scan(r, w, k, v, kk, a):
    B, L, H, N = r.shape

    def step(S, inp):
        r_t, w_t, k_t, v_t, kk_t, a_t = inp
        sa = jnp.einsum('bhvk,bhk->bhv', S, -kk_t)
        S = (S * w_t[:, :, None, :]
             + sa[..., None] * (kk_t * a_t)[:, :, None, :]
             + v_t[..., None] * k_t[:, :, None, :])
        y = jnp.einsum('bhvk,bhk->bhv', S, r_t)
        return S, y

    xs = tuple(jnp.moveaxis(t, 1, 0) for t in (r, w, k, v, kk, a))
    S0 = jnp.zeros((B, H, N, N), jnp.float32)
    _, ys = lax.scan(step, S0, xs)
    return jnp.moveaxis(ys, 0, 1)


def rwkv7_time_mix(p_rw, mu, w0, w_up, a0, a_up, g_up, k_k, k_a, r_k, lnx_g, lnx_b):
    B, L, _ = p_rw.shape
    x = p_rw + (token_shift(p_rw) - p_rw) * mu
    r, k, v, wd, ad, gd = jnp.split(x, RWKV_SPLITS, axis=-1)
    w = jnp.exp(-DECAY_SCALE * jax.nn.sigmoid(w0 + jnp.tanh(wd) @ w_up))
    a = jax.nn.sigmoid(a0 + ad @ a_up)
    g = jax.nn.sigmoid(gd) @ g_up

    def heads(t):
        return t.reshape(B, L, RWKV_HEADS, RWKV_HEAD_DIM)

    kk = heads(k * k_k)
    kk = kk / jnp.maximum(jnp.sqrt(jnp.sum(kk * kk, axis=-1, keepdims=True)), 1e-12)
    k = k * (1.0 + (a - 1.0) * k_a)
    r, k, v, w, a = heads(r), heads(k), heads(v), heads(w), heads(a)
    y = rwkv7_scan(r, w, k, v, kk, a)
    mean = jnp.mean(y, axis=-1, keepdims=True)
    var = jnp.mean(jnp.square(y - mean), axis=-1, keepdims=True)
    yn = ((y - mean) * lax.rsqrt(var + LNX_EPS)).reshape(B, L, RWKV_WIDTH) * lnx_g + lnx_b
    bonus = (jnp.sum(r * k * r_k, axis=-1, keepdims=True) * v).reshape(B, L, RWKV_WIDTH)
    return (yn + bonus) * g


def hybrid_mixer(h, w_in, mu, w0, w_up, a0, a_up, g_up, k_k, k_a, r_k,
                 lnx_g, lnx_b, p_sb, p_rwkv, w_out):
    B, L, _ = h.shape
    p = jnp.einsum('bld,dc->blc', h, w_in).astype(jnp.float32)
    q_sb, k_sb, v_sb, p_rw, gate_sb, gate_rw = jnp.split(p, SPLITS, axis=-1)

    def heads(t):
        return t.reshape(B, L, SB_HEADS, SB_HEAD_DIM).transpose(0, 2, 1, 3)

    o_sb = stick_breaking_attention(heads(q_sb), heads(k_sb), heads(v_sb))
    o_sb = o_sb.transpose(0, 2, 1, 3).reshape(B, L, SB_WIDTH)
    o_rw = rwkv7_time_mix(p_rw, mu, w0, w_up, a0, a_up, g_up, k_k, k_a, r_k, lnx_g, lnx_b)
    merged = (jax.nn.sigmoid(gate_sb) * (o_sb @ p_sb)
              + jax.nn.sigmoid(gate_rw) * (o_rw @ p_rwkv))
    return (merged @ w_out).astype(h.dtype)


def moe_ffn(h, router_w, router_b, wg, wu, wd):
    B, L, D = h.shape
    t = h.reshape(B * L, D)
    logits = (t @ router_w).astype(jnp.float32) + router_b.astype(jnp.float32)
    probs = jax.nn.softmax(logits, axis=-1)
    grouped = probs.reshape(-1, N_GROUPS, EXPERTS_PER_GROUP)
    group_score = jnp.sum(lax.top_k(grouped, TOP_K)[0], axis=-1)
    g_sel = jnp.argmax(group_score, axis=-1)
    in_group = (jnp.arange(N_EXPERTS) // EXPERTS_PER_GROUP)[None, :] == g_sel[:, None]
    top_val, top_idx = lax.top_k(jnp.where(in_group, probs, -1.0), TOP_K)
    gate = top_val / jnp.sum(top_val, axis=-1, keepdims=True)
    combine = jnp.sum(jax.nn.one_hot(top_idx, N_EXPERTS, dtype=jnp.float32) * gate[..., None], axis=1)
    out = jnp.zeros((B * L, D), jnp.float32)
    for e in range(N_EXPERTS):
        hid = jax.nn.silu(t @ wg[e]) * (t @ wu[e])
        out = out + combine[:, e:e + 1] * (hid @ wd[e])
    return out.astype(h.dtype).reshape(B, L, D)


def setup_inputs(seed: int = 0) -> dict:
    key = jax.random.key(seed)
    ks = jax.random.split(key, 32)
    f32 = jnp.float32

    def nrm(k, shape, scale):
        return jax.random.normal(k, shape, f32) * scale

    return {
        "x": nrm(ks[0], (BATCH, SEQ, D_MODEL), 1.0),
        "meta": nrm(ks[1], (N_META, D_MODEL), 1.0),
        "emb_ln_g": 1.0 + nrm(ks[2], (D_MODEL,), 0.02),
        "emb_ln_b": nrm(ks[3], (D_MODEL,), 0.02),
        "w_in": nrm(ks[4], (DEPTH, D_MODEL, C_IN), D_MODEL ** -0.5),
        "rwkv_mu": jax.random.uniform(ks[5], (DEPTH, RWKV_SHIFT_WIDTH), f32),
        "w0": nrm(ks[6], (DEPTH, RWKV_WIDTH), 0.5),
        "w_up": nrm(ks[7], (DEPTH, W_LORA, RWKV_WIDTH), W_LORA ** -0.5),
        "a0": nrm(ks[8], (DEPTH, RWKV_WIDTH), 0.5),
        "a_up": nrm(ks[9], (DEPTH, A_LORA, RWKV_WIDTH), A_LORA ** -0.5),
        "g_up": nrm(ks[10], (DEPTH, G_LORA, RWKV_WIDTH), G_LORA ** -0.5),
        "k_k": 0.85 + nrm(ks[11], (DEPTH, RWKV_WIDTH), 0.05),
        "k_a": 1.0 + nrm(ks[12], (DEPTH, RWKV_WIDTH), 0.05),
        "r_k": nrm(ks[13], (DEPTH, RWKV_HEADS, RWKV_HEAD_DIM), 0.1),
        "lnx_g": 1.0 + nrm(ks[14], (DEPTH, RWKV_WIDTH), 0.02),
        "lnx_b": nrm(ks[15], (DEPTH, RWKV_WIDTH), 0.02),
        "p_sb": nrm(ks[16], (DEPTH, SB_WIDTH, D_MODEL), SB_WIDTH ** -0.5),
        "p_rwkv": nrm(ks[17], (DEPTH, RWKV_WIDTH, D_MODEL), RWKV_WIDTH ** -0.5),
        "w_out": nrm(ks[18], (DEPTH, D_MODEL, D_MODEL), D_MODEL ** -0.5 * BETA_INIT),
        "ln1_g": 1.0 + nrm(ks[19], (DEPTH, D_MODEL), 0.02),
        "ln1_b": nrm(ks[20], (DEPTH, D_MODEL), 0.02),
        "router_w": nrm(ks[21], (D_MODEL, N_EXPERTS), D_MODEL ** -0.5),
        "router_b": nrm(ks[22], (N_EXPERTS,), 0.01),
        "exp_w_gate": nrm(ks[23], (DEPTH, N_EXPERTS, D_MODEL, D_EXPERT), D_MODEL ** -0.5),
        "exp_w_up": nrm(ks[24], (DEPTH, N_EXPERTS, D_MODEL, D_EXPERT), D_MODEL ** -0.5),
        "exp_w_down": nrm(ks[25], (DEPTH, N_EXPERTS, D_EXPERT, D_MODEL), D_EXPERT ** -0.5 * BETA_INIT),
        "ln2_g": 1.0 + nrm(ks[26], (DEPTH, D_MODEL), 0.02),
        "ln2_b": nrm(ks[27], (DEPTH, D_MODEL), 0.02),
    }


def reference(x, meta, emb_ln_g, emb_ln_b, w_in, rwkv_mu, w0, w_up, a0, a_up, g_up,
              k_k, k_a, r_k, lnx_g, lnx_b, p_sb, p_rwkv, w_out, ln1_g, ln1_b,
              router_w, router_b, exp_w_gate, exp_w_up, exp_w_down, ln2_g, ln2_b):
    B = x.shape[0]
    meta_b = jnp.broadcast_to(meta.astype(x.dtype)[None], (B, N_META, D_MODEL))
    h = layer_norm(jnp.concatenate([meta_b, x], axis=1), emb_ln_g, emb_ln_b)
    for l in range(DEPTH):
        mix = hybrid_mixer(h, w_in[l], rwkv_mu[l], w0[l], w_up[l], a0[l], a_up[l], g_up[l],
                           k_k[l], k_a[l], r_k[l], lnx_g[l], lnx_b[l], p_sb[l], p_rwkv[l], w_out[l])
        h = layer_norm(ALPHA * h + mix, ln1_g[l], ln1_b[l])
        ff = moe_ffn(h, router_w, router_b, exp_w_gate[l], exp_w_up[l], exp_w_down[l])
        h = layer_norm(ALPHA * h + ff, ln2_g[l], ln2_b[l])
    return h[:, N_META:]
```

```python
import functools
import math

import jax
import jax.numpy as jnp
from jax import lax
from jax.experimental import pallas as pl
from jax.experimental.pallas import tpu as pltpu

F32 = jnp.float32
BF16 = jnp.bfloat16

D_MODEL = 1024
DEPTH = 2
N_META = 16
HEAD_DIM = 64
PAIR = 2 * HEAD_DIM
MIX_WIDTH = 512
W_LORA, A_LORA, G_LORA = 64, 64, 128
RW_WIDTH = 3 * MIX_WIDTH + W_LORA + A_LORA + G_LORA
DECAY_SCALE = math.exp(-0.5)
LNX_EPS = 64e-5
LN_EPS = 1e-5
N_EXPERTS = 16
N_GROUPS = 4
GROUP = N_EXPERTS // N_GROUPS
D_EXPERT = 512
ALPHA = (2 * DEPTH) ** 0.25

SEQ_ALIGN = 640
ROW_TILE = 640
ATT_BLOCK = 128
SCAN_CHUNK = 64
EXP_ZERO_BELOW = -104.0
VMEM_LIMIT = 48 * 1024 * 1024


def _cparams(sem):
    return pltpu.CompilerParams(dimension_semantics=sem, vmem_limit_bytes=VMEM_LIMIT)


def _split2(x):
    hi = x.astype(BF16)
    lo = (x - hi.astype(F32)).astype(BF16)
    return hi, lo


def _split3(x):
    hi = x.astype(BF16)
    r1 = x - hi.astype(F32)
    mid = r1.astype(BF16)
    lo = (r1 - mid.astype(F32)).astype(BF16)
    return hi, mid, lo


_NN = (((1,), (0,)), ((), ()))
_NT = (((1,), (1,)), ((), ()))


def _dg(a, b, dims):
    return lax.dot_general(a, b, dims, preferred_element_type=F32)


def _mm(a, b, dims=_NN):
    ah, al = _split2(a)
    bh, bl = _split2(b)
    return _dg(ah, bh, dims) + (_dg(ah, bl, dims) + _dg(al, bh, dims))


def _mm_exact_lhs(a_bf16, b, dims=_NN):
    b1, b2, b3 = _split3(b)
    return _dg(a_bf16, b1, dims) + (_dg(a_bf16, b2, dims) + _dg(a_bf16, b3, dims))


def _mm_exact_rhs(a, b_bf16, dims=_NN):
    a1, a2 = _split2(a)
    return _dg(a1, b_bf16, dims) + _dg(a2, b_bf16, dims)


def _sigmoid(x):
    return 1.0 / (1.0 + jnp.exp(-x))


def _layer_norm(x, g, b):
    mu = jnp.mean(x, axis=-1, keepdims=True)
    xc = x - mu
    var = jnp.mean(xc * xc, axis=-1, keepdims=True)
    return xc * lax.rsqrt(var + LN_EPS) * g + b


def _embed_ln_kernel(x_ref, g_ref, b_ref, o_ref):
    o_ref[...] = _layer_norm(x_ref[...], g_ref[...], b_ref[...])


def _embed_ln(x, g, b):
    t, d = x.shape
    return pl.pallas_call(
        _embed_ln_kernel,
        out_shape=jax.ShapeDtypeStruct((t, d), F32),
        grid=(t // ROW_TILE,),
        in_specs=[pl.BlockSpec((ROW_TILE, d), lambda i: (i, 0)),
                  pl.BlockSpec((1, d), lambda i: (0, 0)),
                  pl.BlockSpec((1, d), lambda i: (0, 0))],
        out_specs=pl.BlockSpec((ROW_TILE, d), lambda i: (i, 0)),
        compiler_params=_cparams(("parallel",)),
        name="embed_ln",
    )(x, g, b)


def _qkv_kernel(h_ref, w_ref, q_ref, k_ref, v_ref):
    x = h_ref[...].astype(BF16)
    p = jnp.dot(x, w_ref[...], preferred_element_type=F32)
    q_ref[...] = (p[:, :MIX_WIDTH] * (HEAD_DIM ** -0.5)).astype(BF16)
    k_ref[...] = p[:, MIX_WIDTH:2 * MIX_WIDTH].astype(BF16)
    v_ref[...] = p[:, 2 * MIX_WIDTH:].astype(BF16)


def _qkv_proj(h, w_qkv):
    t, d = h.shape
    out = jax.ShapeDtypeStruct((t, MIX_WIDTH), BF16)
    row = pl.BlockSpec((ROW_TILE, MIX_WIDTH), lambda i: (i, 0))
    return pl.pallas_call(
        _qkv_kernel,
        out_shape=(out, out, out),
        grid=(t // ROW_TILE,),
        in_specs=[pl.BlockSpec((ROW_TILE, d), lambda i: (i, 0)),
                  pl.BlockSpec((d, 3 * MIX_WIDTH), lambda i: (0, 0))],
        out_specs=(row, row, row),
        compiler_params=_cparams(("parallel",)),
        name="qkv_proj",
    )(h, w_qkv)


def _sb_attn_kernel(q_ref, k_ref, v_ref, o_ref):
    blk = ATT_BLOCK
    i = pl.program_id(2)
    q = q_ref[0]
    lane = lax.broadcasted_iota(jnp.int32, (1, PAIR), 1)
    first = lane < HEAD_DIM
    zero = jnp.zeros_like(q)
    q_heads = (jnp.where(first, q, zero), jnp.where(first, zero, q))
    rr = lax.broadcasted_iota(jnp.int32, (blk, blk), 0)
    cc = lax.broadcasted_iota(jnp.int32, (blk, blk), 1)
    causal = cc < rr
    later = (rr > cc).astype(BF16)

    def sweep(j, carry, diagonal):
        o, c0, c1 = carry
        start = pl.multiple_of(j * blk, blk)
        kb = k_ref[0, pl.ds(start, blk), :]
        vb = v_ref[0, pl.ds(start, blk), :]
        pvs, cs = [], []
        for qh, c in zip(q_heads, (c0, c1)):
            z = _dg(qh, kb, _NT)
            sp = jnp.maximum(z, 0.0) + jnp.log1p(jnp.exp(-jnp.abs(z)))
            u = -sp
            if diagonal:
                u = jnp.where(causal, u, 0.0)
            rest = _mm_exact_rhs(u, later) + c
            a = jnp.exp((z - sp) + rest)
            if diagonal:
                a = jnp.where(causal, a, 0.0)
            pvs.append(jnp.dot(a.astype(BF16), vb, preferred_element_type=F32))
            cs.append(c + jnp.sum(u, axis=1, keepdims=True))
        o = o + jnp.where(first, pvs[0], pvs[1])
        return o, cs[0], cs[1]

    zc = jnp.zeros((blk, 1), F32)
    o, c0, c1 = sweep(i, (jnp.zeros((blk, PAIR), F32), zc, zc), True)

    def live(c0, c1):
        return jnp.max(jnp.maximum(c0, c1)) > EXP_ZERO_BELOW

    def cond(st):
        j, _, _, _, alive = st
        return jnp.logical_and(j >= 0, alive)

    def body(st):
        j, o, c0, c1, _ = st
        o, c0, c1 = sweep(j, (o, c0, c1), False)
        return j - 1, o, c0, c1, live(c0, c1)

    _, o, _, _, _ = lax.while_loop(cond, body, (i - 1, o, c0, c1, live(c0, c1)))
    o_ref[0] = o.astype(o_ref.dtype)


def _sb_attention(q, k, v):
    b, l, w = q.shape
    npair = w // PAIR
    return pl.pallas_call(
        _sb_attn_kernel,
        out_shape=jax.ShapeDtypeStruct((b, l, w), BF16),
        grid=(b, npair, l // ATT_BLOCK),
        in_specs=[pl.BlockSpec((1, ATT_BLOCK, PAIR), lambda bi, p, i: (bi, i, p)),
                  pl.BlockSpec((1, l, PAIR), lambda bi, p, i: (bi, 0, p)),
                  pl.BlockSpec((1, l, PAIR), lambda bi, p, i: (bi, 0, p))],
        out_specs=pl.BlockSpec((1, ATT_BLOCK, PAIR), lambda bi, p, i: (bi, i, p)),
        compiler_params=_cparams(("parallel", "parallel", "arbitrary")),
        name="sb_attention",
    )(q, k, v)


def _rwkv_prep_kernel(h_ref, w_ref, mu_ref, w0_ref, wup_ref, a0_ref, aup_ref, gup_ref,
                      kk_ref, ka_ref, bd_ref,
                      r_o, lw_o, k_o, v_o, kn_o, b_o, g_o, prev_sc):
    t = pl.program_id(1)
    x = h_ref[0].astype(BF16)
    p = jnp.dot(x, w_ref[...], preferred_element_type=F32)
    rows = p.shape[0]

    @pl.when(t == 0)
    def _():
        prev_sc[...] = jnp.zeros_like(prev_sc)

    row = lax.broadcasted_iota(jnp.int32, (rows, 1), 0)
    shifted = jnp.where(row == 0, prev_sc[...], pltpu.roll(p, 1, axis=0))
    prev_sc[...] = p[rows - 1:rows, :]
    xs = p + (shifted - p) * mu_ref[...]

    w = MIX_WIDTH
    r, k, v = xs[:, :w], xs[:, w:2 * w], xs[:, 2 * w:3 * w]
    wd = xs[:, 3 * w:3 * w + W_LORA]
    ad = xs[:, 3 * w + W_LORA:3 * w + W_LORA + A_LORA]
    gd = xs[:, 3 * w + W_LORA + A_LORA:]

    lw = -DECAY_SCALE * _sigmoid(w0_ref[...] + _mm(jnp.tanh(wd), wup_ref[...]))
    a = _sigmoid(a0_ref[...] + _mm(ad, aup_ref[...]))
    g = jnp.dot(_sigmoid(gd).astype(BF16), gup_ref[...].astype(BF16), preferred_element_type=F32)

    kk = k * kk_ref[...]
    ssq = _mm_exact_rhs(kk * kk, bd_ref[...])
    kn = kk / jnp.maximum(jnp.sqrt(ssq), 1e-12)
    r_o[0] = r
    lw_o[0] = lw
    k_o[0] = k * (1.0 + (a - 1.0) * ka_ref[...])
    v_o[0] = v
    kn_o[0] = kn
    b_o[0] = kn * a
    g_o[0] = g


def _rwkv_prep(h3, w_rw, mu, w0, w_up, a0, a_up, g_up, k_k, k_a, bd512):
    b, l, d = h3.shape
    nt = l // ROW_TILE
    out = jax.ShapeDtypeStruct((b, l, MIX_WIDTH), F32)
    row = pl.BlockSpec((1, ROW_TILE, MIX_WIDTH), lambda bi, t: (bi, t, 0))

    def full(arr):
        return pl.BlockSpec(arr.shape, lambda bi, t: (0,) * arr.ndim)

    consts = (w_rw, mu, w0, w_up, a0, a_up, g_up, k_k, k_a, bd512)
    return pl.pallas_call(
        _rwkv_prep_kernel,
        out_shape=(out,) * 7,
        grid=(b, nt),
        in_specs=[pl.BlockSpec((1, ROW_TILE, d), lambda bi, t: (bi, t, 0))] + [full(c) for c in consts],
        out_specs=(row,) * 7,
        scratch_shapes=[pltpu.VMEM((1, RW_WIDTH), F32)],
        compiler_params=_cparams(("parallel", "arbitrary")),
        name="rwkv_prep",
    )(h3, *consts)


def _rwkv_scan_kernel(r_ref, lw_ref, k_ref, v_ref, kn_ref, b_ref, g_ref,
                      rk_ref, lng_ref, lnb_ref, o_ref, state_sc):
    nb = r_ref.shape[0]
    c = r_ref.shape[1]
    npair = r_ref.shape[2] // PAIR

    @pl.when(pl.program_id(0) == 0)
    def _():
        state_sc[...] = jnp.zeros_like(state_sc)

    rr = lax.broadcasted_iota(jnp.int32, (c, c), 0)
    cc = lax.broadcasted_iota(jnp.int32, (c, c), 1)
    incl = rr >= cc
    strict = rr > cc
    eye_c = (rr == cc).astype(F32)
    ones_incl = incl.astype(BF16)
    lane = lax.broadcasted_iota(jnp.int32, (1, PAIR), 1)
    first = lane < HEAD_DIM
    pr = lax.broadcasted_iota(jnp.int32, (PAIR, PAIR), 0)
    pc = lax.broadcasted_iota(jnp.int32, (PAIR, PAIR), 1)
    same_head = (pr < HEAD_DIM) == (pc < HEAD_DIM)
    head_ones = same_head.astype(BF16)
    eye_p = pr == pc

    for bi in range(nb):
        for p in range(npair):
            sl = (bi, slice(None), slice(p * PAIR, (p + 1) * PAIR))
            csl = (slice(None), slice(p * PAIR, (p + 1) * PAIR))
            r, lw, k, v = r_ref[sl], lw_ref[sl], k_ref[sl], v_ref[sl]
            kn, b, g = kn_ref[sl], b_ref[sl], g_ref[sl]

            cl = _mm_exact_lhs(ones_incl, lw)
            cl_end = cl[c - 1:c, :]
            a_t = -kn * jnp.exp(cl - lw)
            e_neg = jnp.exp(-cl)
            b_t = b * e_neg
            k_t = k * e_neg
            r_t = r * jnp.exp(cl)
            e_end = jnp.exp(cl_end - cl)
            b_h = b * e_end
            k_h = k * e_end

            halves = []
            for keep in (first, jnp.logical_not(first)):
                a_m = jnp.where(keep, a_t, 0.0)
                r_m = jnp.where(keep, r_t, 0.0)
                l_ab = jnp.where(strict, _mm(a_m, b_t, _NT), 0.0)
                l_ak = jnp.where(strict, _mm(a_m, k_t, _NT), 0.0)
                m_rb = jnp.where(incl, _mm(r_m, b_t, _NT), 0.0)
                m_rk = jnp.where(incl, _mm(r_m, k_t, _NT), 0.0)
                tinv = eye_c + l_ab
                pw = l_ab
                n = 2
                while n < c:
                    pw = _mm(pw, pw)
                    tinv = tinv + _mm(tinv, pw)
                    n *= 2
                pm = _mm(tinv, a_t)
                qm = _mm(tinv, _mm(l_ak, v))
                ry = r_t + _mm(m_rb, pm)
                y0 = _mm(m_rb, qm) + _mm(m_rk, v)
                halves.append((pm, qm, ry, y0))
            pm, qm, ry, y0 = (jnp.where(first, x0, x1) for x0, x1 in zip(*halves))

            b_ht = b_h.T
            gmat = jnp.where(same_head, _mm(b_ht, pm), 0.0) + jnp.where(eye_p, jnp.exp(cl_end), 0.0)
            jmat = jnp.where(same_head, _mm(b_ht, qm) + _mm(k_h.T, v), 0.0)
            idx = bi * npair + p
            h0 = state_sc[idx]
            y = _mm(ry, h0) + y0
            state_sc[idx] = _mm(gmat, h0) + jmat

            mean = _mm_exact_rhs(y, head_ones) * (1.0 / HEAD_DIM)
            yc = y - mean
            var = _mm_exact_rhs(yc * yc, head_ones) * (1.0 / HEAD_DIM)
            yn = yc * lax.rsqrt(var + LNX_EPS) * lng_ref[csl] + lnb_ref[csl]
            bonus = _mm_exact_rhs(r * k * rk_ref[csl], head_ones) * v
            o_ref[sl] = ((yn + bonus) * g).astype(o_ref.dtype)


def _rwkv_scan(r, lw, k, v, kn, bb, g, r_k, lnx_g, lnx_b):
    b, l, w = r.shape
    c = SCAN_CHUNK
    row = pl.BlockSpec((b, c, w), lambda i: (0, i, 0))
    vec = pl.BlockSpec((1, w), lambda i: (0, 0))
    return pl.pallas_call(
        _rwkv_scan_kernel,
        out_shape=jax.ShapeDtypeStruct((b, l, w), BF16),
        grid=(l // c,),
        in_specs=[row] * 7 + [vec] * 3,
        out_specs=row,
        scratch_shapes=[pltpu.VMEM((b * (w // PAIR), PAIR, PAIR), F32)],
        compiler_params=_cparams(("arbitrary",)),
        name="rwkv_scan",
    )(r, lw, k, v, kn, bb, g, r_k, lnx_g, lnx_b)


def _merge_kernel(h_ref, osb_ref, orw_ref, wg_ref, psb_ref, prw_ref, wout_ref,
                  g1_ref, b1_ref, rw_ref, rb_ref, h1_ref, comb_ref):
    h = h_ref[...]
    hb = h.astype(BF16)
    gates = _sigmoid(jnp.dot(hb, wg_ref[...], preferred_element_type=F32))
    sb = jnp.dot(osb_ref[...], psb_ref[...], preferred_element_type=F32)
    rw = jnp.dot(orw_ref[...], prw_ref[...], preferred_element_type=F32)
    merged = gates[:, :D_MODEL] * sb + gates[:, D_MODEL:] * rw
    mix = jnp.dot(merged.astype(BF16), wout_ref[...], preferred_element_type=F32)
    h1 = _layer_norm(ALPHA * h + mix, g1_ref[...], b1_ref[...])
    h1_ref[...] = h1

    logits = _mm(h1, rw_ref[...]) + rb_ref[...]
    m = jnp.max(logits, axis=-1, keepdims=True)
    ex = jnp.exp(logits - m)
    probs = ex / jnp.sum(ex, axis=-1, keepdims=True)
    lane = lax.broadcasted_iota(jnp.int32, probs.shape, 1)
    big = N_EXPERTS

    def top2(vals):
        v1 = jnp.max(vals, axis=-1, keepdims=True)
        i1 = jnp.min(jnp.where(vals == v1, lane, big), axis=-1, keepdims=True)
        rest = jnp.where(lane == i1, -2.0, vals)
        v2 = jnp.max(rest, axis=-1, keepdims=True)
        i2 = jnp.min(jnp.where(rest == v2, lane, big), axis=-1, keepdims=True)
        return v1, i1, v2, i2

    best = None
    sel = None
    for gi in range(N_GROUPS):
        in_g = (lane >= gi * GROUP) & (lane < (gi + 1) * GROUP)
        v1, _, v2, _ = top2(jnp.where(in_g, probs, -1.0))
        score = v1 + v2
        if gi == 0:
            best, sel = score, jnp.zeros_like(score, dtype=jnp.int32)
        else:
            better = score > best
            best = jnp.where(better, score, best)
            sel = jnp.where(better, gi, sel)
    in_sel = (lane >= sel * GROUP) & (lane < (sel + 1) * GROUP)
    v1, i1, v2, i2 = top2(jnp.where(in_sel, probs, -1.0))
    denom = v1 + v2
    comb_ref[...] = jnp.where(lane == i1, v1 / denom, 0.0) + jnp.where(lane == i2, v2 / denom, 0.0)


def _merge(h, o_sb, o_rw, w_gates, p_sb, p_rwkv, w_out, ln_g, ln_b, router_w, router_b):
    t, d = h.shape

    def full(arr):
        return pl.BlockSpec(arr.shape, lambda i: (0,) * arr.ndim)

    consts = (w_gates, p_sb, p_rwkv, w_out, ln_g, ln_b, router_w, router_b)
    return pl.pallas_call(
        _merge_kernel,
        out_shape=(jax.ShapeDtypeStruct((t, d), F32), jax.ShapeDtypeStruct((t, N_EXPERTS), F32)),
        grid=(t // ROW_TILE,),
        in_specs=[pl.BlockSpec((ROW_TILE, d), lambda i: (i, 0)),
                  pl.BlockSpec((ROW_TILE, MIX_WIDTH), lambda i: (i, 0)),
                  pl.BlockSpec((ROW_TILE, MIX_WIDTH), lambda i: (i, 0))] + [full(c) for c in consts],
        out_specs=(pl.BlockSpec((ROW_TILE, d), lambda i: (i, 0)),
                   pl.BlockSpec((ROW_TILE, N_EXPERTS), lambda i: (i, 0))),
        compiler_params=_cparams(("parallel",)),
        name="merge_ln1_router",
    )(h, o_sb, o_rw, *consts)


def _moe_kernel(h_ref, comb_ref, wg_ref, wu_ref, wd_ref, g2_ref, b2_ref, o_ref, acc_sc):
    e = pl.program_id(1)

    @pl.when(e == 0)
    def _():
        acc_sc[...] = jnp.zeros_like(acc_sc)

    x = h_ref[...].astype(BF16)
    gate = jnp.dot(x, wg_ref[0], preferred_element_type=F32)
    up = jnp.dot(x, wu_ref[0], preferred_element_type=F32)
    hid = gate * _sigmoid(gate) * up
    comb = comb_ref[...]
    lane = lax.broadcasted_iota(jnp.int32, comb.shape, 1)
    ce = jnp.sum(jnp.where(lane == e, comb, 0.0), axis=-1, keepdims=True)
    acc_sc[...] += ce * jnp.dot(hid.astype(BF16), wd_ref[0], preferred_element_type=F32)

    @pl.when(e == pl.num_programs(1) - 1)
    def _():
        o_ref[...] = _layer_norm(ALPHA * h_ref[...] + acc_sc[...], g2_ref[...], b2_ref[...])


def _moe(h, comb, wg, wu, wd, ln_g, ln_b):
    t, d = h.shape
    ne, _, f = wg.shape
    return pl.pallas_call(
        _moe_kernel,
        out_shape=jax.ShapeDtypeStruct((t, d), F32),
        grid=(t // ROW_TILE, ne),
        in_specs=[pl.BlockSpec((ROW_TILE, d), lambda i, e: (i, 0)),
                  pl.BlockSpec((ROW_TILE, ne), lambda i, e: (i, 0)),
                  pl.BlockSpec((1, d, f), lambda i, e: (e, 0, 0)),
                  pl.BlockSpec((1, d, f), lambda i, e: (e, 0, 0)),
                  pl.BlockSpec((1, f, d), lambda i, e: (e, 0, 0)),
                  pl.BlockSpec((1, d), lambda i, e: (0, 0)),
                  pl.BlockSpec((1, d), lambda i, e: (0, 0))],
        out_specs=pl.BlockSpec((ROW_TILE, d), lambda i, e: (i, 0)),
        scratch_shapes=[pltpu.VMEM((ROW_TILE, d), F32)],
        compiler_params=_cparams(("parallel", "arbitrary")),
        name="moe_ln2",
    )(h, comb, wg, wu, wd, ln_g, ln_b)


def _layer(h, lp, shared, batch):
    t, d = h.shape
    l = t // batch
    q, k, v = _qkv_proj(h, lp["w_qkv"])
    shp = (batch, l, MIX_WIDTH)
    o_sb = _sb_attention(q.reshape(shp), k.reshape(shp), v.reshape(shp)).reshape(t, MIX_WIDTH)
    prep = _rwkv_prep(h.reshape(batch, l, d), lp["w_rw"], lp["mu"], lp["w0"], lp["w_up"], lp["a0"],
                      lp["a_up"], lp["g_up"], lp["k_k"], lp["k_a"], shared["bd512"])
    o_rw = _rwkv_scan(*prep, lp["r_k"], lp["lnx_g"], lp["lnx_b"]).reshape(t, MIX_WIDTH)
    h1, comb = _merge(h, o_sb, o_rw, lp["w_gates"], lp["p_sb"], lp["p_rwkv"], lp["w_out"],
                      lp["ln1_g"], lp["ln1_b"], shared["router_w"], shared["router_b"])
    return _moe(h1, comb, lp["wg"], lp["wu"], lp["wd"], lp["ln2_g"], lp["ln2_b"])


def kernel(x, meta, emb_ln_g, emb_ln_b, w_in, rwkv_mu, w0, w_up, a0, a_up, g_up, k_k, k_a, r_k, lnx_g, lnx_b, p_sb, p_rwkv, w_out, ln1_g, ln1_b, router_w, router_b, exp_w_gate, exp_w_up, exp_w_down, ln2_g, ln2_b):
    batch, seq, d = x.shape
    n_meta = meta.shape[0]
    l = seq + n_meta
    lpad = -(-l // SEQ_ALIGN) * SEQ_ALIGN
    depth = w_in.shape[0]

    meta_b = jnp.broadcast_to(meta.astype(x.dtype)[None], (batch, n_meta, d))
    hcat = jnp.concatenate([meta_b, x, jnp.zeros((batch, lpad - l, d), x.dtype)], axis=1)
    h = _embed_ln(hcat.reshape(batch * lpad, d), emb_ln_g.reshape(1, d), emb_ln_b.reshape(1, d))

    idx = jnp.arange(MIX_WIDTH) // HEAD_DIM
    shared = {
        "bd512": (idx[:, None] == idx[None, :]).astype(BF16),
        "router_w": router_w,
        "router_b": router_b.reshape(1, -1),
    }
    qkv_w = 3 * MIX_WIDTH
    for li in range(depth):
        row = lambda a: a[li].reshape(1, -1)
        lp = {
            "w_qkv": w_in[li, :, :qkv_w].astype(BF16),
            "w_rw": w_in[li, :, qkv_w:qkv_w + RW_WIDTH].astype(BF16),
            "w_gates": w_in[li, :, qkv_w + RW_WIDTH:].astype(BF16),
            "mu": row(rwkv_mu), "w0": row(w0), "a0": row(a0), "k_k": row(k_k), "k_a": row(k_a),
            "w_up": w_up[li], "a_up": a_up[li], "g_up": g_up[li],
            "r_k": row(r_k), "lnx_g": row(lnx_g), "lnx_b": row(lnx_b),
            "p_sb": p_sb[li].astype(BF16), "p_rwkv": p_rwkv[li].astype(BF16),
            "w_out": w_out[li].astype(BF16),
            "ln1_g": row(ln1_g), "ln1_b": row(ln1_b), "ln2_g": row(ln2_g), "ln2_b": row(ln2_b),
            "wg": exp_w_gate[li].astype(BF16), "wu": exp_w_up[li].astype(BF16),
            "wd": exp_w_down[li].astype(BF16),
        }
        h = _layer(h, lp, shared, batch)
    return h.reshape(batch, lpad, d)[:, n_meta:l]
```

```python
import functools
import math

import jax
import jax.numpy as jnp
from jax import lax
from jax.experimental import pallas as pl
from jax.experimental.pallas import tpu as pltpu

F32 = jnp.float32
BF16 = jnp.bfloat16

D_MODEL = 1024
DEPTH = 2
N_META = 16
HEAD_DIM = 64
PAIR = 2 * HEAD_DIM
MIX_WIDTH = 512
W_LORA, A_LORA, G_LORA = 64, 64, 128
RW_WIDTH = 3 * MIX_WIDTH + W_LORA + A_LORA + G_LORA
DECAY_SCALE = math.exp(-0.5)
LNX_EPS = 64e-5
LN_EPS = 1e-5
N_EXPERTS = 16
N_GROUPS = 4
GROUP = N_EXPERTS // N_GROUPS
D_EXPERT = 512
ALPHA = (2 * DEPTH) ** 0.25

SEQ_ALIGN = 640
ROW_TILE = 640
ATT_BLOCK = 128
SCAN_CHUNK = 64
EXP_ZERO_BELOW = -104.0
VMEM_LIMIT = 48 * 1024 * 1024


def _cparams(sem):
    return pltpu.CompilerParams(dimension_semantics=sem, vmem_limit_bytes=VMEM_LIMIT)


def _split2(x):
    hi = x.astype(BF16)
    lo = (x - hi.astype(F32)).astype(BF16)
    return hi, lo


def _split3(x):
    hi = x.astype(BF16)
    r1 = x - hi.astype(F32)
    mid = r1.astype(BF16)
    lo = (r1 - mid.astype(F32)).astype(BF16)
    return hi, mid, lo


_NN = (((1,), (0,)), ((), ()))
_NT = (((1,), (1,)), ((), ()))


def _dg(a, b, dims):
    return lax.dot_general(a, b, dims, preferred_element_type=F32)


def _mm(a, b, dims=_NN):
    ah, al = _split2(a)
    bh, bl = _split2(b)
    return _dg(ah, bh, dims) + (_dg(ah, bl, dims) + _dg(al, bh, dims))


def _mm_exact_lhs(a_bf16, b, dims=_NN):
    b1, b2, b3 = _split3(b)
    return _dg(a_bf16, b1, dims) + (_dg(a_bf16, b2, dims) + _dg(a_bf16, b3, dims))


def _mm_exact_rhs(a, b_bf16, dims=_NN):
    a1, a2 = _split2(a)
    return _dg(a1, b_bf16, dims) + _dg(a2, b_bf16, dims)


def _sigmoid(x):
    return 1.0 / (1.0 + jnp.exp(-x))


def _layer_norm(x, g, b):
    mu = jnp.mean(x, axis=-1, keepdims=True)
    xc = x - mu
    var = jnp.mean(xc * xc, axis=-1, keepdims=True)
    return xc * lax.rsqrt(var + LN_EPS) * g + b


def _embed_ln_kernel(x_ref, g_ref, b_ref, o_ref):
    o_ref[...] = _layer_norm(x_ref[...], g_ref[...], b_ref[...])


def _embed_ln(x, g, b):
    t, d = x.shape
    return pl.pallas_call(
        _embed_ln_kernel,
        out_shape=jax.ShapeDtypeStruct((t, d), F32),
        grid=(t // ROW_TILE,),
        in_specs=[pl.BlockSpec((ROW_TILE, d), lambda i: (i, 0)),
                  pl.BlockSpec((1, d), lambda i: (0, 0)),
                  pl.BlockSpec((1, d), lambda i: (0, 0))],
        out_specs=pl.BlockSpec((ROW_TILE, d), lambda i: (i, 0)),
        compiler_params=_cparams(("parallel",)),
        name="embed_ln",
    )(x, g, b)


def _qkv_kernel(h_ref, w_ref, q_ref, k_ref, v_ref):
    x = h_ref[...].astype(BF16)
    p = jnp.dot(x, w_ref[...], preferred_element_type=F32)
    q_ref[...] = (p[:, :MIX_WIDTH] * (HEAD_DIM ** -0.5)).astype(BF16)
    k_ref[...] = p[:, MIX_WIDTH:2 * MIX_WIDTH].astype(BF16)
    v_ref[...] = p[:, 2 * MIX_WIDTH:].astype(BF16)


def _qkv_proj(h, w_qkv):
    t, d = h.shape
    out = jax.ShapeDtypeStruct((t, MIX_WIDTH), BF16)
    row = pl.BlockSpec((ROW_TILE, MIX_WIDTH), lambda i: (i, 0))
    return pl.pallas_call(
        _qkv_kernel,
        out_shape=(out, out, out),
        grid=(t // ROW_TILE,),
        in_specs=[pl.BlockSpec((ROW_TILE, d), lambda i: (i, 0)),
                  pl.BlockSpec((d, 3 * MIX_WIDTH), lambda i: (0, 0))],
        out_specs=(row, row, row),
        compiler_params=_cparams(("parallel",)),
        name="qkv_proj",
    )(h, w_qkv)


def _sb_attn_kernel(q_ref, k_ref, v_ref, o_ref):
    blk = ATT_BLOCK
    i = pl.program_id(2)
    q = q_ref[0]
    lane = lax.broadcasted_iota(jnp.int32, (1, PAIR), 1)
    first = lane < HEAD_DIM
    zero = jnp.zeros_like(q)
    q_heads = (jnp.where(first, q, zero), jnp.where(first, zero, q))
    rr = lax.broadcasted_iota(jnp.int32, (blk, blk), 0)
    cc = lax.broadcasted_iota(jnp.int32, (blk, blk), 1)
    causal = cc < rr
    later = (rr > cc).astype(BF16)

    def sweep(j, carry, diagonal):
        o, c0, c1 = carry
        start = pl.multiple_of(j * blk, blk)
        kb = k_ref[0, pl.ds(start, blk), :]
        vb = v_ref[0, pl.ds(start, blk), :]
        pvs, cs = [], []
        for qh, c in zip(q_heads, (c0, c1)):
            z = _dg(qh, kb, _NT)
            sp = jnp.maximum(z, 0.0) + jnp.log1p(jnp.exp(-jnp.abs(z)))
            u = -sp
            if diagonal:
                u = jnp.where(causal, u, 0.0)
            rest = _mm_exact_rhs(u, later) + c
            a = jnp.exp((z - sp) + rest)
            if diagonal:
                a = jnp.where(causal, a, 0.0)
            pvs.append(jnp.dot(a.astype(BF16), vb, preferred_element_type=F32))
            cs.append(c + jnp.sum(u, axis=1, keepdims=True))
        o = o + jnp.where(first, pvs[0], pvs[1])
        return o, cs[0], cs[1]

    zc = jnp.zeros((blk, 1), F32)
    o, c0, c1 = sweep(i, (jnp.zeros((blk, PAIR), F32), zc, zc), True)

    def live(c0, c1):
        return jnp.max(jnp.maximum(c0, c1)) > EXP_ZERO_BELOW

    def cond(st):
        j, _, _, _, alive = st
        return jnp.logical_and(j >= 0, alive)

    def body(st):
        j, o, c0, c1, _ = st
        o, c0, c1 = sweep(j, (o, c0, c1), False)
        return j - 1, o, c0, c1, live(c0, c1)

    _, o, _, _, _ = lax.while_loop(cond, body, (i - 1, o, c0, c1, live(c0, c1)))
    o_ref[0] = o.astype(o_ref.dtype)


def _sb_attention(q, k, v):
    b, l, w = q.shape
    npair = w // PAIR
    return pl.pallas_call(
        _sb_attn_kernel,
        out_shape=jax.ShapeDtypeStruct((b, l, w), BF16),
        grid=(b, npair, l // ATT_BLOCK),
        in_specs=[pl.BlockSpec((1, ATT_BLOCK, PAIR), lambda bi, p, i: (bi, i, p)),
                  pl.BlockSpec((1, l, PAIR), lambda bi, p, i: (bi, 0, p)),
                  pl.BlockSpec((1, l, PAIR), lambda bi, p, i: (bi, 0, p))],
        out_specs=pl.BlockSpec((1, ATT_BLOCK, PAIR), lambda bi, p, i: (bi, i, p)),
        compiler_params=_cparams(("parallel", "parallel", "arbitrary")),
        name="sb_attention",
    )(q, k, v)


def _rwkv_prep_kernel(h_ref, w_ref, mu_ref, w0_ref, wup_ref, a0_ref, aup_ref, gup_ref,
                      kk_ref, ka_ref, bd_ref,
                      r_o, lw_o, k_o, v_o, kn_o, b_o, g_o, prev_sc):
    t = pl.program_id(1)
    x = h_ref[0].astype(BF16)
    p = jnp.dot(x, w_ref[...], preferred_element_type=F32)
    rows = p.shape[0]

    @pl.when(t == 0)
    def _():
        prev_sc[...] = jnp.zeros_like(prev_sc)

    row = lax.broadcasted_iota(jnp.int32, (rows, 1), 0)
    shifted = jnp.where(row == 0, prev_sc[...], pltpu.roll(p, 1, axis=0))
    prev_sc[...] = p[rows - 1:rows, :]
    xs = p + (shifted - p) * mu_ref[...]

    w = MIX_WIDTH
    r, k, v = xs[:, :w], xs[:, w:2 * w], xs[:, 2 * w:3 * w]
    wd = xs[:, 3 * w:3 * w + W_LORA]
    ad = xs[:, 3 * w + W_LORA:3 * w + W_LORA + A_LORA]
    gd = xs[:, 3 * w + W_LORA + A_LORA:]

    lw = -DECAY_SCALE * _sigmoid(w0_ref[...] + _mm(jnp.tanh(wd), wup_ref[...]))
    a = _sigmoid(a0_ref[...] + _mm(ad, aup_ref[...]))
    g = jnp.dot(_sigmoid(gd).astype(BF16), gup_ref[...].astype(BF16), preferred_element_type=F32)

    kk = k * kk_ref[...]
    ssq = _mm_exact_rhs(kk * kk, bd_ref[...])
    kn = kk / jnp.maximum(jnp.sqrt(ssq), 1e-12)
    r_o[0] = r
    lw_o[0] = lw
    k_o[0] = k * (1.0 + (a - 1.0) * ka_ref[...])
    v_o[0] = v
    kn_o[0] = kn
    b_o[0] = kn * a
    g_o[0] = g


def _rwkv_prep(h3, w_rw, mu, w0, w_up, a0, a_up, g_up, k_k, k_a, bd512):
    b, l, d = h3.shape
    nt = l // ROW_TILE
    out = jax.ShapeDtypeStruct((b, l, MIX_WIDTH), F32)
    row = pl.BlockSpec((1, ROW_TILE, MIX_WIDTH), lambda bi, t: (bi, t, 0))

    def full(arr):
        return pl.BlockSpec(arr.shape, lambda bi, t: (0,) * arr.ndim)

    consts = (w_rw, mu, w0, w_up, a0, a_up, g_up, k_k, k_a, bd512)
    return pl.pallas_call(
        _rwkv_prep_kernel,
        out_shape=(out,) * 7,
        grid=(b, nt),
        in_specs=[pl.BlockSpec((1, ROW_TILE, d), lambda bi, t: (bi, t, 0))] + [full(c) for c in consts],
        out_specs=(row,) * 7,
        scratch_shapes=[pltpu.VMEM((1, RW_WIDTH), F32)],
        compiler_params=_cparams(("parallel", "arbitrary")),
        name="rwkv_prep",
    )(h3, *consts)


def _rwkv_scan_kernel(r_ref, lw_ref, k_ref, v_ref, kn_ref, b_ref, g_ref,
                      rk_ref, lng_ref, lnb_ref, o_ref, state_sc):
    nb, c, w = r_ref.shape
    npair = w // PAIR
    assert 2 * c == PAIR

    @pl.when(pl.program_id(0) == 0)
    def _():
        state_sc[...] = jnp.zeros_like(state_sc)

    rr = lax.broadcasted_iota(jnp.int32, (c, c), 0)
    cc = lax.broadcasted_iota(jnp.int32, (c, c), 1)
    ones_incl = (rr >= cc).astype(BF16)
    row_c = lax.broadcasted_iota(jnp.int32, (c, PAIR), 0)
    key_c = lax.broadcasted_iota(jnp.int32, (c, PAIR), 1) & (HEAD_DIM - 1)
    strict_m = key_c < row_c
    incl_m = key_c <= row_c
    first = lax.broadcasted_iota(jnp.int32, (1, PAIR), 1) < HEAD_DIM
    pr = lax.broadcasted_iota(jnp.int32, (PAIR, PAIR), 0)
    pc = lax.broadcasted_iota(jnp.int32, (PAIR, PAIR), 1)
    same_head = (pr < HEAD_DIM) == (pc < HEAD_DIM)
    head0_rows = lax.broadcasted_iota(jnp.int32, (PAIR, 2 * PAIR), 0) < HEAD_DIM
    head_ones = same_head.astype(BF16)
    eye_p = pr == pc
    eye_f = eye_p.astype(F32)
    zeros_c = jnp.zeros((c, PAIR), F32)

    def cat0(*xs):
        return jnp.concatenate(xs, axis=0)

    def cat1(*xs):
        return jnp.concatenate(xs, axis=1)

    def mm(a, b, dims=_NN):
        return _dg(a.astype(BF16), b.astype(BF16), dims)

    streams = [(bi, p) for bi in range(nb) for p in range(npair)]
    pre = {}
    for bi in range(nb):
        lw = lw_ref[bi]
        cl = _mm_exact_lhs(ones_incl, lw)
        cl_end = cl[c - 1:c, :]
        e_pos = jnp.exp(cl)
        e_neg = jnp.exp(-cl)
        e_end = jnp.exp(cl_end - cl)
        r, k, kn, b = r_ref[bi], k_ref[bi], kn_ref[bi], b_ref[bi]
        full = dict(a_t=-kn * jnp.exp(cl - lw), r_t=r * e_pos, b_t=b * e_neg, k_t=k * e_neg,
                    b_h=b * e_end, k_h=k * e_end, g_end=jnp.exp(cl_end))
        for p in range(npair):
            pre[(bi, p)] = {n: x[:, p * PAIR:(p + 1) * PAIR] for n, x in full.items()}

    tri = {}
    for s in streams:
        d = pre[s]
        a_t, r_t, b_t, k_t = d["a_t"], d["r_t"], d["b_t"], d["k_t"]
        bk = cat0(b_t, k_t)
        s0 = mm(jnp.where(first, cat0(a_t, r_t), 0.0), bk, _NT)
        s1t = mm(jnp.where(first, 0.0, a_t), cat0(k_t, b_t), _NT)
        s1b = mm(jnp.where(first, 0.0, r_t), bk, _NT)
        top0 = jnp.where(strict_m, s0[:c], 0.0)
        top1 = jnp.where(strict_m, s1t, 0.0)
        tri[s] = dict(
            l_ab=cat0(jnp.where(first, top0, 0.0), jnp.where(first, 0.0, top1)),
            l_ak=cat0(jnp.where(first, 0.0, top0), jnp.where(first, top1, 0.0)),
            m0=jnp.where(incl_m, s0[c:], 0.0), m1=jnp.where(incl_m, s1b, 0.0))

    xs = {}
    for s in streams:
        bi, p = s
        v = v_ref[bi, :, p * PAIR:(p + 1) * PAIR]
        xs[s] = mm(tri[s]["l_ak"], cat0(v, v))

    tinv = {s: eye_f + tri[s]["l_ab"] for s in streams}
    pw = {s: tri[s]["l_ab"] for s in streams}
    n = 2
    while n < c:
        for s in streams:
            pw[s] = mm(pw[s], pw[s])
        for s in streams:
            tinv[s] = tinv[s] + mm(tinv[s], pw[s])
        n *= 2

    pq = {}
    for s in streams:
        a_t = pre[s]["a_t"]
        rhs = cat0(cat1(a_t, xs[s][:c]), cat1(a_t, xs[s][c:]))
        pq[s] = mm(tinv[s], rhs)

    for s in streams:
        bi, p = s
        d = pre[s]
        sl = (bi, slice(None), slice(p * PAIR, (p + 1) * PAIR))
        csl = (slice(None), slice(p * PAIR, (p + 1) * PAIR))
        v = v_ref[sl]
        bkh_t = cat0(d["b_h"], d["k_h"]).T
        zv = cat1(zeros_c, v)
        res0 = mm(cat0(tri[s]["m0"], bkh_t), cat0(pq[s][:c], zv))
        res1 = mm(cat0(tri[s]["m1"], bkh_t), cat0(pq[s][c:], zv))
        ry = d["r_t"] + jnp.where(first, res0[:c, :PAIR], res1[:c, :PAIR])
        y0 = jnp.where(first, res0[:c, PAIR:], res1[:c, PAIR:])
        gj = jnp.where(head0_rows, res0[c:], res1[c:])
        gmat = jnp.where(same_head, gj[:, :PAIR], 0.0) + jnp.where(eye_p, d["g_end"], 0.0)
        jmat = jnp.where(same_head, gj[:, PAIR:], 0.0)
        idx = bi * npair + p
        st = mm(cat0(ry, gmat), state_sc[idx])
        y = st[:c] + y0
        state_sc[idx] = st[c:] + jmat

        r, k = r_ref[sl], k_ref[sl]
        mean = _mm_exact_rhs(y, head_ones) * (1.0 / HEAD_DIM)
        yc = y - mean
        var = _mm_exact_rhs(yc * yc, head_ones) * (1.0 / HEAD_DIM)
        yn = yc * lax.rsqrt(var + LNX_EPS) * lng_ref[csl] + lnb_ref[csl]
        bonus = _mm_exact_rhs(r * k * rk_ref[csl], head_ones) * v
        o_ref[sl] = ((yn + bonus) * g_ref[sl]).astype(o_ref.dtype)


def _rwkv_scan(r, lw, k, v, kn, bb, g, r_k, lnx_g, lnx_b):
    b, l, w = r.shape
    c = SCAN_CHUNK
    row = pl.BlockSpec((b, c, w), lambda i: (0, i, 0))
    vec = pl.BlockSpec((1, w), lambda i: (0, 0))
    return pl.pallas_call(
        _rwkv_scan_kernel,
        out_shape=jax.ShapeDtypeStruct((b, l, w), BF16),
        grid=(l // c,),
        in_specs=[row] * 7 + [vec] * 3,
        out_specs=row,
        scratch_shapes=[pltpu.VMEM((b * (w // PAIR), PAIR, PAIR), F32)],
        compiler_params=_cparams(("arbitrary",)),
        name="rwkv_scan",
    )(r, lw, k, v, kn, bb, g, r_k, lnx_g, lnx_b)


def _merge_kernel(h_ref, osb_ref, orw_ref, wg_ref, psb_ref, prw_ref, wout_ref,
                  g1_ref, b1_ref, rw_ref, rb_ref, h1_ref, comb_ref):
    h = h_ref[...]
    hb = h.astype(BF16)
    gates = _sigmoid(jnp.dot(hb, wg_ref[...], preferred_element_type=F32))
    sb = jnp.dot(osb_ref[...], psb_ref[...], preferred_element_type=F32)
    rw = jnp.dot(orw_ref[...], prw_ref[...], preferred_element_type=F32)
    merged = gates[:, :D_MODEL] * sb + gates[:, D_MODEL:] * rw
    mix = jnp.dot(merged.astype(BF16), wout_ref[...], preferred_element_type=F32)
    h1 = _layer_norm(ALPHA * h + mix, g1_ref[...], b1_ref[...])
    h1_ref[...] = h1

    logits = _mm(h1, rw_ref[...]) + rb_ref[...]
    m = jnp.max(logits, axis=-1, keepdims=True)
    ex = jnp.exp(logits - m)
    probs = ex / jnp.sum(ex, axis=-1, keepdims=True)
    lane = lax.broadcasted_iota(jnp.int32, probs.shape, 1)
    big = N_EXPERTS

    def top2(vals):
        v1 = jnp.max(vals, axis=-1, keepdims=True)
        i1 = jnp.min(jnp.where(vals == v1, lane, big), axis=-1, keepdims=True)
        rest = jnp.where(lane == i1, -2.0, vals)
        v2 = jnp.max(rest, axis=-1, keepdims=True)
        i2 = jnp.min(jnp.where(rest == v2, lane, big), axis=-1, keepdims=True)
        return v1, i1, v2, i2

    best = None
    sel = None
    for gi in range(N_GROUPS):
        in_g = (lane >= gi * GROUP) & (lane < (gi + 1) * GROUP)
        v1, _, v2, _ = top2(jnp.where(in_g, probs, -1.0))
        score = v1 + v2
        if gi == 0:
            best, sel = score, jnp.zeros_like(score, dtype=jnp.int32)
        else:
            better = score > best
            best = jnp.where(better, score, best)
            sel = jnp.where(better, gi, sel)
    in_sel = (lane >= sel * GROUP) & (lane < (sel + 1) * GROUP)
    v1, i1, v2, i2 = top2(jnp.where(in_sel, probs, -1.0))
    denom = v1 + v2
    comb_ref[...] = jnp.where(lane == i1, v1 / denom, 0.0) + jnp.where(lane == i2, v2 / denom, 0.0)


def _merge(h, o_sb, o_rw, w_gates, p_sb, p_rwkv, w_out, ln_g, ln_b, router_w, router_b):
    t, d = h.shape

    def full(arr):
        return pl.BlockSpec(arr.shape, lambda i: (0,) * arr.ndim)

    consts = (w_gates, p_sb, p_rwkv, w_out, ln_g, ln_b, router_w, router_b)
    return pl.pallas_call(
        _merge_kernel,
        out_shape=(jax.ShapeDtypeStruct((t, d), F32), jax.ShapeDtypeStruct((t, N_EXPERTS), F32)),
        grid=(t // ROW_TILE,),
        in_specs=[pl.BlockSpec((ROW_TILE, d), lambda i: (i, 0)),
                  pl.BlockSpec((ROW_TILE, MIX_WIDTH), lambda i: (i, 0)),
                  pl.BlockSpec((ROW_TILE, MIX_WIDTH), lambda i: (i, 0))] + [full(c) for c in consts],
        out_specs=(pl.BlockSpec((ROW_TILE, d), lambda i: (i, 0)),
                   pl.BlockSpec((ROW_TILE, N_EXPERTS), lambda i: (i, 0))),
        compiler_params=_cparams(("parallel",)),
        name="merge_ln1_router",
    )(h, o_sb, o_rw, *consts)


def _moe_kernel(h_ref, comb_ref, wg_ref, wu_ref, wd_ref, g2_ref, b2_ref, o_ref, acc_sc):
    e = pl.program_id(1)

    @pl.when(e == 0)
    def _():
        acc_sc[...] = jnp.zeros_like(acc_sc)

    x = h_ref[...].astype(BF16)
    gate = jnp.dot(x, wg_ref[0], preferred_element_type=F32)
    up = jnp.dot(x, wu_ref[0], preferred_element_type=F32)
    hid = gate * _sigmoid(gate) * up
    comb = comb_ref[...]
    lane = lax.broadcasted_iota(jnp.int32, comb.shape, 1)
    ce = jnp.sum(jnp.where(lane == e, comb, 0.0), axis=-1, keepdims=True)
    acc_sc[...] += ce * jnp.dot(hid.astype(BF16), wd_ref[0], preferred_element_type=F32)

    @pl.when(e == pl.num_programs(1) - 1)
    def _():
        o_ref[...] = _layer_norm(ALPHA * h_ref[...] + acc_sc[...], g2_ref[...], b2_ref[...])


def _moe(h, comb, wg, wu, wd, ln_g, ln_b):
    t, d = h.shape
    ne, _, f = wg.shape
    return pl.pallas_call(
        _moe_kernel,
        out_shape=jax.ShapeDtypeStruct((t, d), F32),
        grid=(t // ROW_TILE, ne),
        in_specs=[pl.BlockSpec((ROW_TILE, d), lambda i, e: (i, 0)),
                  pl.BlockSpec((ROW_TILE, ne), lambda i, e: (i, 0)),
                  pl.BlockSpec((1, d, f), lambda i, e: (e, 0, 0)),
                  pl.BlockSpec((1, d, f), lambda i, e: (e, 0, 0)),
                  pl.BlockSpec((1, f, d), lambda i, e: (e, 0, 0)),
                  pl.BlockSpec((1, d), lambda i, e: (0, 0)),
                  pl.BlockSpec((1, d), lambda i, e: (0, 0))],
        out_specs=pl.BlockSpec((ROW_TILE, d), lambda i, e: (i, 0)),
        scratch_shapes=[pltpu.VMEM((ROW_TILE, d), F32)],
        compiler_params=_cparams(("parallel", "arbitrary")),
        name="moe_ln2",
    )(h, comb, wg, wu, wd, ln_g, ln_b)


def _layer(h, lp, shared, batch):
    t, d = h.shape
    l = t // batch
    q, k, v = _qkv_proj(h, lp["w_qkv"])
    shp = (batch, l, MIX_WIDTH)
    o_sb = _sb_attention(q.reshape(shp), k.reshape(shp), v.reshape(shp)).reshape(t, MIX_WIDTH)
    prep = _rwkv_prep(h.reshape(batch, l, d), lp["w_rw"], lp["mu"], lp["w0"], lp["w_up"], lp["a0"],
                      lp["a_up"], lp["g_up"], lp["k_k"], lp["k_a"], shared["bd512"])
    o_rw = _rwkv_scan(*prep, lp["r_k"], lp["lnx_g"], lp["lnx_b"]).reshape(t, MIX_WIDTH)
    h1, comb = _merge(h, o_sb, o_rw, lp["w_gates"], lp["p_sb"], lp["p_rwkv"], lp["w_out"],
                      lp["ln1_g"], lp["ln1_b"], shared["router_w"], shared["router_b"])
    return _moe(h1, comb, lp["wg"], lp["wu"], lp["wd"], lp["ln2_g"], lp["ln2_b"])


def kernel(x, meta, emb_ln_g, emb_ln_b, w_in, rwkv_mu, w0, w_up, a0, a_up, g_up, k_k, k_a, r_k, lnx_g, lnx_b, p_sb, p_rwkv, w_out, ln1_g, ln1_b, router_w, router_b, exp_w_gate, exp_w_up, exp_w_down, ln2_g, ln2_b):
    batch, seq, d = x.shape
    n_meta = meta.shape[0]
    l = seq + n_meta
    lpad = -(-l // SEQ_ALIGN) * SEQ_ALIGN
    depth = w_in.shape[0]

    meta_b = jnp.broadcast_to(meta.astype(x.dtype)[None], (batch, n_meta, d))
    hcat = jnp.concatenate([meta_b, x, jnp.zeros((batch, lpad - l, d), x.dtype)], axis=1)
    h = _embed_ln(hcat.reshape(batch * lpad, d), emb_ln_g.reshape(1, d), emb_ln_b.reshape(1, d))

    idx = jnp.arange(MIX_WIDTH) // HEAD_DIM
    shared = {
        "bd512": (idx[:, None] == idx[None, :]).astype(BF16),
        "router_w": router_w,
        "router_b": router_b.reshape(1, -1),
    }
    qkv_w = 3 * MIX_WIDTH
    for li in range(depth):
        row = lambda a: a[li].reshape(1, -1)
        lp = {
            "w_qkv": w_in[li, :, :qkv_w].astype(BF16),
            "w_rw": w_in[li, :, qkv_w:qkv_w + RW_WIDTH].astype(BF16),
            "w_gates": w_in[li, :, qkv_w + RW_WIDTH:].astype(BF16),
            "mu": row(rwkv_mu), "w0": row(w0), "a0": row(a0), "k_k": row(k_k), "k_a": row(k_a),
            "w_up": w_up[li], "a_up": a_up[li], "g_up": g_up[li],
            "r_k": row(r_k), "lnx_g": row(lnx_g), "lnx_b": row(lnx_b),
            "p_sb": p_sb[li].astype(BF16), "p_rwkv": p_rwkv[li].astype(BF16),
            "w_out": w_out[li].astype(BF16),
            "ln1_g": row(ln1_g), "ln1_b": row(ln1_b), "ln2_g": row(ln2_g), "ln2_b": row(ln2_b),
            "wg": exp_w_gate[li].astype(BF16), "wu": exp_w_up[li].astype(BF16),
            "wd": exp_w_down[li].astype(BF16),
        }
        h = _layer(h, lp, shared, batch)
    return h.reshape(batch, lpad, d)[:, n_meta:l]
```

```python
import functools
import math

import jax
import jax.numpy as jnp
from jax import lax
from jax.experimental import pallas as pl
from jax.experimental.pallas import tpu as pltpu

F32 = jnp.float32
BF16 = jnp.bfloat16

D_MODEL = 1024
DEPTH = 2
N_META = 16
HEAD_DIM = 64
PAIR = 2 * HEAD_DIM
MIX_WIDTH = 512
W_LORA, A_LORA, G_LORA = 64, 64, 128
RW_WIDTH = 3 * MIX_WIDTH + W_LORA + A_LORA + G_LORA
DECAY_SCALE = math.exp(-0.5)
LNX_EPS = 64e-5
LN_EPS = 1e-5
N_EXPERTS = 16
N_GROUPS = 4
GROUP = N_EXPERTS // N_GROUPS
D_EXPERT = 512
ALPHA = (2 * DEPTH) ** 0.25

SEQ_ALIGN = 640
ROW_TILE = 640
ATT_BLOCK = 128
ATT_PAIRS = 4
SCAN_CHUNK = 64
EXP_ZERO_BELOW = -104.0
MOE_ROW_TILE = 1280
VMEM_LIMIT = 48 * 1024 * 1024
MOE_VMEM_LIMIT = 58 * 1024 * 1024


def _cparams(sem):
    return pltpu.CompilerParams(dimension_semantics=sem, vmem_limit_bytes=VMEM_LIMIT)


def _split2(x):
    hi = x.astype(BF16)
    lo = (x - hi.astype(F32)).astype(BF16)
    return hi, lo


def _split3(x):
    hi = x.astype(BF16)
    r1 = x - hi.astype(F32)
    mid = r1.astype(BF16)
    lo = (r1 - mid.astype(F32)).astype(BF16)
    return hi, mid, lo


_NN = (((1,), (0,)), ((), ()))
_NT = (((1,), (1,)), ((), ()))


def _dg(a, b, dims):
    return lax.dot_general(a, b, dims, preferred_element_type=F32)


def _mm(a, b, dims=_NN):
    ah, al = _split2(a)
    bh, bl = _split2(b)
    return _dg(ah, bh, dims) + (_dg(ah, bl, dims) + _dg(al, bh, dims))


def _mm_exact_lhs(a_bf16, b, dims=_NN):
    b1, b2, b3 = _split3(b)
    return _dg(a_bf16, b1, dims) + (_dg(a_bf16, b2, dims) + _dg(a_bf16, b3, dims))


def _mm_exact_rhs(a, b_bf16, dims=_NN):
    a1, a2 = _split2(a)
    return _dg(a1, b_bf16, dims) + _dg(a2, b_bf16, dims)


def _sigmoid(x):
    return 1.0 / (1.0 + jnp.exp(-x))


def _layer_norm(x, g, b):
    mu = jnp.mean(x, axis=-1, keepdims=True)
    xc = x - mu
    var = jnp.mean(xc * xc, axis=-1, keepdims=True)
    return xc * lax.rsqrt(var + LN_EPS) * g + b


def _embed_ln_kernel(x_ref, g_ref, b_ref, o_ref):
    o_ref[...] = _layer_norm(x_ref[...], g_ref[...], b_ref[...])


def _embed_ln(x, g, b):
    t, d = x.shape
    return pl.pallas_call(
        _embed_ln_kernel,
        out_shape=jax.ShapeDtypeStruct((t, d), F32),
        grid=(t // ROW_TILE,),
        in_specs=[pl.BlockSpec((ROW_TILE, d), lambda i: (i, 0)),
                  pl.BlockSpec((1, d), lambda i: (0, 0)),
                  pl.BlockSpec((1, d), lambda i: (0, 0))],
        out_specs=pl.BlockSpec((ROW_TILE, d), lambda i: (i, 0)),
        compiler_params=_cparams(("parallel",)),
        name="embed_ln",
    )(x, g, b)


def _qkv_kernel(h_ref, w_ref, q_ref, k_ref, v_ref):
    x = h_ref[...].astype(BF16)
    p = jnp.dot(x, w_ref[...], preferred_element_type=F32)
    q_ref[...] = (p[:, :MIX_WIDTH] * (HEAD_DIM ** -0.5)).astype(BF16)
    k_ref[...] = p[:, MIX_WIDTH:2 * MIX_WIDTH].astype(BF16)
    v_ref[...] = p[:, 2 * MIX_WIDTH:].astype(BF16)


def _qkv_proj(h, w_qkv):
    t, d = h.shape
    out = jax.ShapeDtypeStruct((t, MIX_WIDTH), BF16)
    row = pl.BlockSpec((ROW_TILE, MIX_WIDTH), lambda i: (i, 0))
    return pl.pallas_call(
        _qkv_kernel,
        out_shape=(out, out, out),
        grid=(t // ROW_TILE,),
        in_specs=[pl.BlockSpec((ROW_TILE, d), lambda i: (i, 0)),
                  pl.BlockSpec((d, 3 * MIX_WIDTH), lambda i: (0, 0))],
        out_specs=(row, row, row),
        compiler_params=_cparams(("parallel",)),
        name="qkv_proj",
    )(h, w_qkv)


def _sb_attn_kernel(q_ref, k_ref, v_ref, o_ref, acc_sc):
    blk = ATT_BLOCK
    npair = q_ref.shape[2] // PAIR
    i = pl.program_id(2)
    first = lax.broadcasted_iota(jnp.int32, (1, PAIR), 1) < HEAD_DIM
    heads = []
    for p in range(npair):
        q = q_ref[0, :, p * PAIR:(p + 1) * PAIR]
        zero = jnp.zeros_like(q)
        heads += [(p, jnp.where(first, q, zero)), (p, jnp.where(first, zero, q))]
    rr = lax.broadcasted_iota(jnp.int32, (blk, blk), 0)
    cc = lax.broadcasted_iota(jnp.int32, (blk, blk), 1)
    causal = cc < rr
    later = (rr > cc).astype(BF16)

    def sweep(j, cs, diagonal):
        start = pl.multiple_of(j * blk, blk)
        kbs = [k_ref[0, pl.ds(start, blk), p * PAIR:(p + 1) * PAIR] for p in range(npair)]
        vbs = [v_ref[0, pl.ds(start, blk), p * PAIR:(p + 1) * PAIR] for p in range(npair)]
        zs = [_dg(qh, kbs[p], _NT) for p, qh in heads]
        sps = [jnp.maximum(z, 0.0) + jnp.log(1.0 + jnp.exp(-jnp.abs(z))) for z in zs]
        us = [jnp.where(causal, sp, 0.0) for sp in sps] if diagonal else sps
        rests = [c - _mm_exact_rhs(u, later) for u, c in zip(us, cs)]
        ws = [jnp.exp((z - sp) + rest) for z, sp, rest in zip(zs, sps, rests)]
        if diagonal:
            ws = [jnp.where(causal, a, 0.0) for a in ws]
        pvs = [jnp.dot(a.astype(BF16), vbs[p], preferred_element_type=F32) for (p, _), a in zip(heads, ws)]
        for p in range(npair):
            pv = jnp.where(first, pvs[2 * p], pvs[2 * p + 1])
            if diagonal:
                acc_sc[:, p * PAIR:(p + 1) * PAIR] = pv
            else:
                acc_sc[:, p * PAIR:(p + 1) * PAIR] += pv
        return tuple(c - jnp.sum(u, axis=1, keepdims=True) for c, u in zip(cs, us))

    def live(cs):
        m = cs[0]
        for c in cs[1:]:
            m = jnp.maximum(m, c)
        return jnp.max(m) > EXP_ZERO_BELOW

    cs = sweep(i, (jnp.zeros((blk, 1), F32),) * len(heads), True)

    def cond(st):
        j, _, alive = st
        return jnp.logical_and(j >= 0, alive)

    def body(st):
        j, cs, _ = st
        cs = sweep(j, cs, False)
        return j - 1, cs, live(cs)

    lax.while_loop(cond, body, (i - 1, cs, live(cs)))
    o_ref[0] = acc_sc[...].astype(o_ref.dtype)


def _sb_attention(q, k, v):
    b, l, w = q.shape
    wb = ATT_PAIRS * PAIR
    return pl.pallas_call(
        _sb_attn_kernel,
        out_shape=jax.ShapeDtypeStruct((b, l, w), BF16),
        grid=(b, w // wb, l // ATT_BLOCK),
        in_specs=[pl.BlockSpec((1, ATT_BLOCK, wb), lambda bi, p, i: (bi, i, p)),
                  pl.BlockSpec((1, l, wb), lambda bi, p, i: (bi, 0, p)),
                  pl.BlockSpec((1, l, wb), lambda bi, p, i: (bi, 0, p))],
        out_specs=pl.BlockSpec((1, ATT_BLOCK, wb), lambda bi, p, i: (bi, i, p)),
        scratch_shapes=[pltpu.VMEM((ATT_BLOCK, wb), F32)],
        compiler_params=_cparams(("parallel", "parallel", "arbitrary")),
        name="sb_attention",
    )(q, k, v)


def _rwkv_prep_kernel(h_ref, w_ref, mu_ref, w0_ref, wup_ref, a0_ref, aup_ref, gup_ref,
                      kk_ref, ka_ref, bd_ref,
                      r_o, lw_o, k_o, v_o, kn_o, b_o, g_o, prev_sc):
    t = pl.program_id(1)
    x = h_ref[0].astype(BF16)
    p = jnp.dot(x, w_ref[...], preferred_element_type=F32)
    rows = p.shape[0]

    @pl.when(t == 0)
    def _():
        prev_sc[...] = jnp.zeros_like(prev_sc)

    row = lax.broadcasted_iota(jnp.int32, (rows, 1), 0)
    shifted = jnp.where(row == 0, prev_sc[...], pltpu.roll(p, 1, axis=0))
    prev_sc[...] = p[rows - 1:rows, :]
    xs = p + (shifted - p) * mu_ref[...]

    w = MIX_WIDTH
    r, k, v = xs[:, :w], xs[:, w:2 * w], xs[:, 2 * w:3 * w]
    wd = xs[:, 3 * w:3 * w + W_LORA]
    ad = xs[:, 3 * w + W_LORA:3 * w + W_LORA + A_LORA]
    gd = xs[:, 3 * w + W_LORA + A_LORA:]

    lw = -DECAY_SCALE * _sigmoid(w0_ref[...] + _mm(jnp.tanh(wd), wup_ref[...]))
    a = _sigmoid(a0_ref[...] + _mm(ad, aup_ref[...]))
    g = jnp.dot(_sigmoid(gd).astype(BF16), gup_ref[...].astype(BF16), preferred_element_type=F32)

    kk = k * kk_ref[...]
    ssq = _mm_exact_rhs(kk * kk, bd_ref[...])
    kn = kk / jnp.maximum(jnp.sqrt(ssq), 1e-12)
    r_o[0] = r
    lw_o[0] = lw
    k_o[0] = k * (1.0 + (a - 1.0) * ka_ref[...])
    v_o[0] = v
    kn_o[0] = kn
    b_o[0] = kn * a
    g_o[0] = g


def _rwkv_prep(h3, w_rw, mu, w0, w_up, a0, a_up, g_up, k_k, k_a, bd512):
    b, l, d = h3.shape
    nt = l // ROW_TILE
    out = jax.ShapeDtypeStruct((b, l, MIX_WIDTH), F32)
    row = pl.BlockSpec((1, ROW_TILE, MIX_WIDTH), lambda bi, t: (bi, t, 0))

    def full(arr):
        return pl.BlockSpec(arr.shape, lambda bi, t: (0,) * arr.ndim)

    consts = (w_rw, mu, w0, w_up, a0, a_up, g_up, k_k, k_a, bd512)
    return pl.pallas_call(
        _rwkv_prep_kernel,
        out_shape=(out,) * 7,
        grid=(b, nt),
        in_specs=[pl.BlockSpec((1, ROW_TILE, d), lambda bi, t: (bi, t, 0))] + [full(c) for c in consts],
        out_specs=(row,) * 7,
        scratch_shapes=[pltpu.VMEM((1, RW_WIDTH), F32)],
        compiler_params=_cparams(("parallel", "arbitrary")),
        name="rwkv_prep",
    )(h3, *consts)


def _rwkv_scan_kernel(r_ref, lw_ref, k_ref, v_ref, kn_ref, b_ref, g_ref,
                      rk_ref, lng_ref, lnb_ref, o_ref, state_sc):
    nb, c, w = r_ref.shape
    npair = w // PAIR
    assert 2 * c == PAIR

    @pl.when(pl.program_id(0) == 0)
    def _():
        state_sc[...] = jnp.zeros_like(state_sc)

    rr = lax.broadcasted_iota(jnp.int32, (c, c), 0)
    cc = lax.broadcasted_iota(jnp.int32, (c, c), 1)
    ones_incl = (rr >= cc).astype(BF16)
    row_c = lax.broadcasted_iota(jnp.int32, (c, PAIR), 0)
    key_c = lax.broadcasted_iota(jnp.int32, (c, PAIR), 1) & (HEAD_DIM - 1)
    strict_m = key_c < row_c
    incl_m = key_c <= row_c
    first = lax.broadcasted_iota(jnp.int32, (1, PAIR), 1) < HEAD_DIM
    pr = lax.broadcasted_iota(jnp.int32, (PAIR, PAIR), 0)
    pc = lax.broadcasted_iota(jnp.int32, (PAIR, PAIR), 1)
    same_head = (pr < HEAD_DIM) == (pc < HEAD_DIM)
    head0_rows = lax.broadcasted_iota(jnp.int32, (PAIR, 2 * PAIR), 0) < HEAD_DIM
    head_ones = same_head.astype(BF16)
    eye_p = pr == pc
    eye_f = eye_p.astype(F32)
    zeros_c = jnp.zeros((c, PAIR), F32)

    def cat0(*xs):
        return jnp.concatenate(xs, axis=0)

    def cat1(*xs):
        return jnp.concatenate(xs, axis=1)

    def mm(a, b, dims=_NN):
        return _dg(a.astype(BF16), b.astype(BF16), dims)

    streams = [(bi, p) for bi in range(nb) for p in range(npair)]
    pre = {}
    for bi in range(nb):
        lw = lw_ref[bi]
        cl = _mm_exact_lhs(ones_incl, lw)
        cl_end = cl[c - 1:c, :]
        e_pos = jnp.exp(cl)
        e_neg = jnp.exp(-cl)
        e_end = jnp.exp(cl_end - cl)
        r, k, kn, b = r_ref[bi], k_ref[bi], kn_ref[bi], b_ref[bi]
        full = dict(a_t=-kn * jnp.exp(cl - lw), r_t=r * e_pos, b_t=b * e_neg, k_t=k * e_neg,
                    b_h=b * e_end, k_h=k * e_end, g_end=jnp.exp(cl_end))
        for p in range(npair):
            pre[(bi, p)] = {n: x[:, p * PAIR:(p + 1) * PAIR] for n, x in full.items()}

    tri = {}
    for s in streams:
        d = pre[s]
        a_t, r_t, b_t, k_t = d["a_t"], d["r_t"], d["b_t"], d["k_t"]
        bk = cat0(b_t, k_t)
        s0 = mm(jnp.where(first, cat0(a_t, r_t), 0.0), bk, _NT)
        s1t = mm(jnp.where(first, 0.0, a_t), cat0(k_t, b_t), _NT)
        s1b = mm(jnp.where(first, 0.0, r_t), bk, _NT)
        top0 = jnp.where(strict_m, s0[:c], 0.0)
        top1 = jnp.where(strict_m, s1t, 0.0)
        tri[s] = dict(
            l_ab=cat0(jnp.where(first, top0, 0.0), jnp.where(first, 0.0, top1)),
            l_ak=cat0(jnp.where(first, 0.0, top0), jnp.where(first, top1, 0.0)),
            m0=jnp.where(incl_m, s0[c:], 0.0), m1=jnp.where(incl_m, s1b, 0.0))

    xs = {}
    for s in streams:
        bi, p = s
        v = v_ref[bi, :, p * PAIR:(p + 1) * PAIR]
        xs[s] = mm(tri[s]["l_ak"], cat0(v, v))

    tinv = {s: eye_f + tri[s]["l_ab"] for s in streams}
    pw = {s: tri[s]["l_ab"] for s in streams}
    n = 2
    while n < c:
        for s in streams:
            pw[s] = mm(pw[s], pw[s])
        for s in streams:
            tinv[s] = tinv[s] + mm(tinv[s], pw[s])
        n *= 2

    pq = {}
    for s in streams:
        a_t = pre[s]["a_t"]
        rhs = cat0(cat1(a_t, xs[s][:c]), cat1(a_t, xs[s][c:]))
        pq[s] = mm(tinv[s], rhs)

    for s in streams:
        bi, p = s
        d = pre[s]
        sl = (bi, slice(None), slice(p * PAIR, (p + 1) * PAIR))
        csl = (slice(None), slice(p * PAIR, (p + 1) * PAIR))
        v = v_ref[sl]
        bkh_t = cat0(d["b_h"], d["k_h"]).T
        zv = cat1(zeros_c, v)
        res0 = mm(cat0(tri[s]["m0"], bkh_t), cat0(pq[s][:c], zv))
        res1 = mm(cat0(tri[s]["m1"], bkh_t), cat0(pq[s][c:], zv))
        ry = d["r_t"] + jnp.where(first, res0[:c, :PAIR], res1[:c, :PAIR])
        y0 = jnp.where(first, res0[:c, PAIR:], res1[:c, PAIR:])
        gj = jnp.where(head0_rows, res0[c:], res1[c:])
        gmat = jnp.where(same_head, gj[:, :PAIR], 0.0) + jnp.where(eye_p, d["g_end"], 0.0)
        jmat = jnp.where(same_head, gj[:, PAIR:], 0.0)
        idx = bi * npair + p
        st = mm(cat0(ry, gmat), state_sc[idx])
        y = st[:c] + y0
        state_sc[idx] = st[c:] + jmat

        r, k = r_ref[sl], k_ref[sl]
        mean = _mm_exact_rhs(y, head_ones) * (1.0 / HEAD_DIM)
        yc = y - mean
        var = _mm_exact_rhs(yc * yc, head_ones) * (1.0 / HEAD_DIM)
        yn = yc * lax.rsqrt(var + LNX_EPS) * lng_ref[csl] + lnb_ref[csl]
        bonus = _mm_exact_rhs(r * k * rk_ref[csl], head_ones) * v
        o_ref[sl] = ((yn + bonus) * g_ref[sl]).astype(o_ref.dtype)


def _rwkv_scan(r, lw, k, v, kn, bb, g, r_k, lnx_g, lnx_b):
    b, l, w = r.shape
    c = SCAN_CHUNK
    row = pl.BlockSpec((b, c, w), lambda i: (0, i, 0))
    vec = pl.BlockSpec((1, w), lambda i: (0, 0))
    return pl.pallas_call(
        _rwkv_scan_kernel,
        out_shape=jax.ShapeDtypeStruct((b, l, w), BF16),
        grid=(l // c,),
        in_specs=[row] * 7 + [vec] * 3,
        out_specs=row,
        scratch_shapes=[pltpu.VMEM((b * (w // PAIR), PAIR, PAIR), F32)],
        compiler_params=_cparams(("arbitrary",)),
        name="rwkv_scan",
    )(r, lw, k, v, kn, bb, g, r_k, lnx_g, lnx_b)


def _merge_kernel(h_ref, osb_ref, orw_ref, wg_ref, psb_ref, prw_ref, wout_ref,
                  g1_ref, b1_ref, rw_ref, rb_ref, h1_ref, comb_ref):
    h = h_ref[...]
    hb = h.astype(BF16)
    gates = _sigmoid(jnp.dot(hb, wg_ref[...], preferred_element_type=F32))
    sb = jnp.dot(osb_ref[...], psb_ref[...], preferred_element_type=F32)
    rw = jnp.dot(orw_ref[...], prw_ref[...], preferred_element_type=F32)
    merged = gates[:, :D_MODEL] * sb + gates[:, D_MODEL:] * rw
    mix = jnp.dot(merged.astype(BF16), wout_ref[...], preferred_element_type=F32)
    h1 = _layer_norm(ALPHA * h + mix, g1_ref[...], b1_ref[...])
    h1_ref[...] = h1

    logits = _mm(h1, rw_ref[...]) + rb_ref[...]
    m = jnp.max(logits, axis=-1, keepdims=True)
    ex = jnp.exp(logits - m)
    probs = ex / jnp.sum(ex, axis=-1, keepdims=True)
    lane = lax.broadcasted_iota(jnp.int32, probs.shape, 1)
    big = N_EXPERTS

    def top2(vals):
        v1 = jnp.max(vals, axis=-1, keepdims=True)
        i1 = jnp.min(jnp.where(vals == v1, lane, big), axis=-1, keepdims=True)
        rest = jnp.where(lane == i1, -2.0, vals)
        v2 = jnp.max(rest, axis=-1, keepdims=True)
        i2 = jnp.min(jnp.where(rest == v2, lane, big), axis=-1, keepdims=True)
        return v1, i1, v2, i2

    best = None
    sel = None
    for gi in range(N_GROUPS):
        in_g = (lane >= gi * GROUP) & (lane < (gi + 1) * GROUP)
        v1, _, v2, _ = top2(jnp.where(in_g, probs, -1.0))
        score = v1 + v2
        if gi == 0:
            best, sel = score, jnp.zeros_like(score, dtype=jnp.int32)
        else:
            better = score > best
            best = jnp.where(better, score, best)
            sel = jnp.where(better, gi, sel)
    in_sel = (lane >= sel * GROUP) & (lane < (sel + 1) * GROUP)
    v1, i1, v2, i2 = top2(jnp.where(in_sel, probs, -1.0))
    denom = v1 + v2
    comb_ref[...] = jnp.where(lane == i1, v1 / denom, 0.0) + jnp.where(lane == i2, v2 / denom, 0.0)


def _merge(h, o_sb, o_rw, w_gates, p_sb, p_rwkv, w_out, ln_g, ln_b, router_w, router_b):
    t, d = h.shape

    def full(arr):
        return pl.BlockSpec(arr.shape, lambda i: (0,) * arr.ndim)

    consts = (w_gates, p_sb, p_rwkv, w_out, ln_g, ln_b, router_w, router_b)
    return pl.pallas_call(
        _merge_kernel,
        out_shape=(jax.ShapeDtypeStruct((t, d), F32), jax.ShapeDtypeStruct((t, N_EXPERTS), F32)),
        grid=(t // ROW_TILE,),
        in_specs=[pl.BlockSpec((ROW_TILE, d), lambda i: (i, 0)),
                  pl.BlockSpec((ROW_TILE, MIX_WIDTH), lambda i: (i, 0)),
                  pl.BlockSpec((ROW_TILE, MIX_WIDTH), lambda i: (i, 0))] + [full(c) for c in consts],
        out_specs=(pl.BlockSpec((ROW_TILE, d), lambda i: (i, 0)),
                   pl.BlockSpec((ROW_TILE, N_EXPERTS), lambda i: (i, 0))),
        compiler_params=_cparams(("parallel",)),
        name="merge_ln1_router",
    )(h, o_sb, o_rw, *consts)


def _moe_kernel(h_ref, comb_ref, wg_ref, wu_ref, wd_ref, g2_ref, b2_ref, o_ref, acc_sc):
    e = pl.program_id(1)

    @pl.when(e == 0)
    def _():
        acc_sc[...] = jnp.zeros_like(acc_sc)

    x = h_ref[...].astype(BF16)
    gate = jnp.dot(x, wg_ref[0, 0].astype(BF16), preferred_element_type=F32)
    up = jnp.dot(x, wu_ref[0, 0].astype(BF16), preferred_element_type=F32)
    hid = gate * _sigmoid(gate) * up
    comb = comb_ref[...]
    lane = lax.broadcasted_iota(jnp.int32, comb.shape, 1)
    ce = jnp.sum(jnp.where(lane == e, comb, 0.0), axis=-1, keepdims=True)
    acc_sc[...] += ce * jnp.dot(hid.astype(BF16), wd_ref[0, 0].astype(BF16), preferred_element_type=F32)

    @pl.when(e == pl.num_programs(1) - 1)
    def _():
        o_ref[...] = _layer_norm(ALPHA * h_ref[...] + acc_sc[...], g2_ref[...], b2_ref[...])


def _moe(h, comb, wg, wu, wd, li, ln_g, ln_b):
    t, d = h.shape
    _, ne, _, f = wg.shape
    rows = MOE_ROW_TILE
    return pl.pallas_call(
        _moe_kernel,
        out_shape=jax.ShapeDtypeStruct((t, d), F32),
        grid=(t // rows, ne),
        in_specs=[pl.BlockSpec((rows, d), lambda i, e: (i, 0), pipeline_mode=pl.Buffered(1)),
                  pl.BlockSpec((rows, ne), lambda i, e: (i, 0)),
                  pl.BlockSpec((1, 1, d, f), lambda i, e: (li, e, 0, 0)),
                  pl.BlockSpec((1, 1, d, f), lambda i, e: (li, e, 0, 0)),
                  pl.BlockSpec((1, 1, f, d), lambda i, e: (li, e, 0, 0)),
                  pl.BlockSpec((1, d), lambda i, e: (0, 0)),
                  pl.BlockSpec((1, d), lambda i, e: (0, 0))],
        out_specs=pl.BlockSpec((rows, d), lambda i, e: (i, 0)),
        scratch_shapes=[pltpu.VMEM((rows, d), F32)],
        compiler_params=pltpu.CompilerParams(dimension_semantics=("parallel", "arbitrary"),
                                             vmem_limit_bytes=MOE_VMEM_LIMIT),
        name="moe_ln2",
    )(h, comb, wg, wu, wd, ln_g, ln_b)


def _layer(h, lp, shared, batch):
    t, d = h.shape
    l = t // batch
    q, k, v = _qkv_proj(h, lp["w_qkv"])
    shp = (batch, l, MIX_WIDTH)
    o_sb = _sb_attention(q.reshape(shp), k.reshape(shp), v.reshape(shp)).reshape(t, MIX_WIDTH)
    prep = _rwkv_prep(h.reshape(batch, l, d), lp["w_rw"], lp["mu"], lp["w0"], lp["w_up"], lp["a0"],
                      lp["a_up"], lp["g_up"], lp["k_k"], lp["k_a"], shared["bd512"])
    o_rw = _rwkv_scan(*prep, lp["r_k"], lp["lnx_g"], lp["lnx_b"]).reshape(t, MIX_WIDTH)
    h1, comb = _merge(h, o_sb, o_rw, lp["w_gates"], lp["p_sb"], lp["p_rwkv"], lp["w_out"],
                      lp["ln1_g"], lp["ln1_b"], shared["router_w"], shared["router_b"])
    return _moe(h1, comb, shared["wg"], shared["wu"], shared["wd"], lp["li"], lp["ln2_g"], lp["ln2_b"])


def kernel(x, meta, emb_ln_g, emb_ln_b, w_in, rwkv_mu, w0, w_up, a0, a_up, g_up, k_k, k_a, r_k, lnx_g, lnx_b, p_sb, p_rwkv, w_out, ln1_g, ln1_b, router_w, router_b, exp_w_gate, exp_w_up, exp_w_down, ln2_g, ln2_b):
    batch, seq, d = x.shape
    n_meta = meta.shape[0]
    l = seq + n_meta
    lpad = -(-l // SEQ_ALIGN) * SEQ_ALIGN
    depth = w_in.shape[0]

    meta_b = jnp.broadcast_to(meta.astype(x.dtype)[None], (batch, n_meta, d))
    hcat = jnp.concatenate([meta_b, x, jnp.zeros((batch, lpad - l, d), x.dtype)], axis=1)
    h = _embed_ln(hcat.reshape(batch * lpad, d), emb_ln_g.reshape(1, d), emb_ln_b.reshape(1, d))

    idx = jnp.arange(MIX_WIDTH) // HEAD_DIM
    shared = {
        "bd512": (idx[:, None] == idx[None, :]).astype(BF16),
        "router_w": router_w,
        "wg": exp_w_gate, "wu": exp_w_up, "wd": exp_w_down,
        "router_b": router_b.reshape(1, -1),
    }
    qkv_w = 3 * MIX_WIDTH
    for li in range(depth):
        row = lambda a: a[li].reshape(1, -1)
        lp = {
            "w_qkv": w_in[li, :, :qkv_w].astype(BF16),
            "w_rw": w_in[li, :, qkv_w:qkv_w + RW_WIDTH].astype(BF16),
            "w_gates": w_in[li, :, qkv_w + RW_WIDTH:].astype(BF16),
            "mu": row(rwkv_mu), "w0": row(w0), "a0": row(a0), "k_k": row(k_k), "k_a": row(k_a),
            "w_up": w_up[li], "a_up": a_up[li], "g_up": g_up[li],
            "r_k": row(r_k), "lnx_g": row(lnx_g), "lnx_b": row(lnx_b),
            "p_sb": p_sb[li].astype(BF16), "p_rwkv": p_rwkv[li].astype(BF16),
            "w_out": w_out[li].astype(BF16),
            "ln1_g": row(ln1_g), "ln1_b": row(ln1_b), "ln2_g": row(ln2_g), "ln2_b": row(ln2_b),
            "li": li,
        }
        h = _layer(h, lp, shared, batch)
    return h.reshape(batch, lpad, d)[:, n_meta:l]
```

```python
import functools
import math

import jax
import jax.numpy as jnp
from jax import lax
from jax.experimental import pallas as pl
from jax.experimental.pallas import tpu as pltpu

F32 = jnp.float32
BF16 = jnp.bfloat16

D_MODEL = 1024
DEPTH = 2
N_META = 16
HEAD_DIM = 64
PAIR = 2 * HEAD_DIM
MIX_WIDTH = 512
W_LORA, A_LORA, G_LORA = 64, 64, 128
RW_WIDTH = 3 * MIX_WIDTH + W_LORA + A_LORA + G_LORA
DECAY_SCALE = math.exp(-0.5)
LNX_EPS = 64e-5
LN_EPS = 1e-5
N_EXPERTS = 16
N_GROUPS = 4
GROUP = N_EXPERTS // N_GROUPS
D_EXPERT = 512
ALPHA = (2 * DEPTH) ** 0.25

SEQ_ALIGN = 640
ROW_TILE = 640
ATT_BLOCK = 128
ATT_PAIRS = 4
SCAN_CHUNK = 64
SCAN_CHUNKS_PER_STEP = 2
EXP_ZERO_BELOW = -104.0
MOE_ROW_TILE = 1280
VMEM_LIMIT = 48 * 1024 * 1024
MOE_VMEM_LIMIT = 58 * 1024 * 1024


def _cparams(sem):
    return pltpu.CompilerParams(dimension_semantics=sem, vmem_limit_bytes=VMEM_LIMIT)


def _split2(x):
    hi = x.astype(BF16)
    lo = (x - hi.astype(F32)).astype(BF16)
    return hi, lo


def _split3(x):
    hi = x.astype(BF16)
    r1 = x - hi.astype(F32)
    mid = r1.astype(BF16)
    lo = (r1 - mid.astype(F32)).astype(BF16)
    return hi, mid, lo


_NN = (((1,), (0,)), ((), ()))
_NT = (((1,), (1,)), ((), ()))


def _dg(a, b, dims):
    return lax.dot_general(a, b, dims, preferred_element_type=F32)


def _mm(a, b, dims=_NN):
    ah, al = _split2(a)
    bh, bl = _split2(b)
    return _dg(ah, bh, dims) + (_dg(ah, bl, dims) + _dg(al, bh, dims))


def _mm_exact_lhs(a_bf16, b, dims=_NN):
    b1, b2, b3 = _split3(b)
    return _dg(a_bf16, b1, dims) + (_dg(a_bf16, b2, dims) + _dg(a_bf16, b3, dims))


def _mm_exact_rhs(a, b_bf16, dims=_NN):
    a1, a2 = _split2(a)
    return _dg(a1, b_bf16, dims) + _dg(a2, b_bf16, dims)


def _sigmoid(x):
    return 1.0 / (1.0 + jnp.exp(-x))


def _layer_norm(x, g, b):
    mu = jnp.mean(x, axis=-1, keepdims=True)
    xc = x - mu
    var = jnp.mean(xc * xc, axis=-1, keepdims=True)
    return xc * lax.rsqrt(var + LN_EPS) * g + b


def _embed_ln_kernel(x_ref, g_ref, b_ref, o_ref):
    o_ref[...] = _layer_norm(x_ref[...], g_ref[...], b_ref[...])


def _embed_ln(x, g, b):
    t, d = x.shape
    return pl.pallas_call(
        _embed_ln_kernel,
        out_shape=jax.ShapeDtypeStruct((t, d), F32),
        grid=(t // ROW_TILE,),
        in_specs=[pl.BlockSpec((ROW_TILE, d), lambda i: (i, 0)),
                  pl.BlockSpec((1, d), lambda i: (0, 0)),
                  pl.BlockSpec((1, d), lambda i: (0, 0))],
        out_specs=pl.BlockSpec((ROW_TILE, d), lambda i: (i, 0)),
        compiler_params=_cparams(("parallel",)),
        name="embed_ln",
    )(x, g, b)


def _qkv_kernel(h_ref, w_ref, q_ref, k_ref, v_ref):
    x = h_ref[...].astype(BF16)
    p = jnp.dot(x, w_ref[...], preferred_element_type=F32)
    q_ref[...] = (p[:, :MIX_WIDTH] * (HEAD_DIM ** -0.5)).astype(BF16)
    k_ref[...] = p[:, MIX_WIDTH:2 * MIX_WIDTH].astype(BF16)
    v_ref[...] = p[:, 2 * MIX_WIDTH:].astype(BF16)


def _qkv_proj(h, w_qkv):
    t, d = h.shape
    out = jax.ShapeDtypeStruct((t, MIX_WIDTH), BF16)
    row = pl.BlockSpec((ROW_TILE, MIX_WIDTH), lambda i: (i, 0))
    return pl.pallas_call(
        _qkv_kernel,
        out_shape=(out, out, out),
        grid=(t // ROW_TILE,),
        in_specs=[pl.BlockSpec((ROW_TILE, d), lambda i: (i, 0)),
                  pl.BlockSpec((d, 3 * MIX_WIDTH), lambda i: (0, 0))],
        out_specs=(row, row, row),
        compiler_params=_cparams(("parallel",)),
        name="qkv_proj",
    )(h, w_qkv)


def _sb_attn_kernel(q_ref, k_ref, v_ref, o_ref, acc_sc):
    blk = ATT_BLOCK
    npair = q_ref.shape[2] // PAIR
    i = pl.program_id(2)
    first = lax.broadcasted_iota(jnp.int32, (1, PAIR), 1) < HEAD_DIM
    heads = []
    for p in range(npair):
        q = q_ref[0, :, p * PAIR:(p + 1) * PAIR]
        zero = jnp.zeros_like(q)
        heads += [(p, jnp.where(first, q, zero)), (p, jnp.where(first, zero, q))]
    rr = lax.broadcasted_iota(jnp.int32, (blk, blk), 0)
    cc = lax.broadcasted_iota(jnp.int32, (blk, blk), 1)
    causal = cc < rr
    later = (rr > cc).astype(BF16)

    def sweep(j, cs, diagonal):
        start = pl.multiple_of(j * blk, blk)
        kbs = [k_ref[0, pl.ds(start, blk), p * PAIR:(p + 1) * PAIR] for p in range(npair)]
        vbs = [v_ref[0, pl.ds(start, blk), p * PAIR:(p + 1) * PAIR] for p in range(npair)]
        zs = [_dg(qh, kbs[p], _NT) for p, qh in heads]
        sps = [jnp.maximum(z, 0.0) + jnp.log(1.0 + jnp.exp(-jnp.abs(z))) for z in zs]
        us = [jnp.where(causal, sp, 0.0) for sp in sps] if diagonal else sps
        rests = [c - _mm_exact_rhs(u, later) for u, c in zip(us, cs)]
        ws = [jnp.exp((z - sp) + rest) for z, sp, rest in zip(zs, sps, rests)]
        if diagonal:
            ws = [jnp.where(causal, a, 0.0) for a in ws]
        pvs = [jnp.dot(a.astype(BF16), vbs[p], preferred_element_type=F32) for (p, _), a in zip(heads, ws)]
        for p in range(npair):
            pv = jnp.where(first, pvs[2 * p], pvs[2 * p + 1])
            if diagonal:
                acc_sc[:, p * PAIR:(p + 1) * PAIR] = pv
            else:
                acc_sc[:, p * PAIR:(p + 1) * PAIR] += pv
        return tuple(c - jnp.sum(u, axis=1, keepdims=True) for c, u in zip(cs, us))

    def live(cs):
        m = cs[0]
        for c in cs[1:]:
            m = jnp.maximum(m, c)
        return jnp.max(m) > EXP_ZERO_BELOW

    cs = sweep(i, (jnp.zeros((blk, 1), F32),) * len(heads), True)

    def cond(st):
        j, _, alive = st
        return jnp.logical_and(j >= 0, alive)

    def body(st):
        j, cs, _ = st
        cs = sweep(j, cs, False)
        return j - 1, cs, live(cs)

    lax.while_loop(cond, body, (i - 1, cs, live(cs)))
    o_ref[0] = acc_sc[...].astype(o_ref.dtype)


def _sb_attention(q, k, v):
    b, l, w = q.shape
    wb = ATT_PAIRS * PAIR
    return pl.pallas_call(
        _sb_attn_kernel,
        out_shape=jax.ShapeDtypeStruct((b, l, w), BF16),
        grid=(b, w // wb, l // ATT_BLOCK),
        in_specs=[pl.BlockSpec((1, ATT_BLOCK, wb), lambda bi, p, i: (bi, i, p)),
                  pl.BlockSpec((1, l, wb), lambda bi, p, i: (bi, 0, p)),
                  pl.BlockSpec((1, l, wb), lambda bi, p, i: (bi, 0, p))],
        out_specs=pl.BlockSpec((1, ATT_BLOCK, wb), lambda bi, p, i: (bi, i, p)),
        scratch_shapes=[pltpu.VMEM((ATT_BLOCK, wb), F32)],
        compiler_params=_cparams(("parallel", "parallel", "arbitrary")),
        name="sb_attention",
    )(q, k, v)


def _rwkv_prep_kernel(h_ref, w_ref, mu_ref, w0_ref, wup_ref, a0_ref, aup_ref, gup_ref,
                      kk_ref, ka_ref, bd_ref,
                      r_o, lw_o, k_o, v_o, kn_o, b_o, g_o, prev_sc):
    t = pl.program_id(1)
    x = h_ref[0].astype(BF16)
    p = jnp.dot(x, w_ref[...], preferred_element_type=F32)
    rows = p.shape[0]

    @pl.when(t == 0)
    def _():
        prev_sc[...] = jnp.zeros_like(prev_sc)

    row = lax.broadcasted_iota(jnp.int32, (rows, 1), 0)
    shifted = jnp.where(row == 0, prev_sc[...], pltpu.roll(p, 1, axis=0))
    prev_sc[...] = p[rows - 1:rows, :]
    xs = p + (shifted - p) * mu_ref[...]

    w = MIX_WIDTH
    r, k, v = xs[:, :w], xs[:, w:2 * w], xs[:, 2 * w:3 * w]
    wd = xs[:, 3 * w:3 * w + W_LORA]
    ad = xs[:, 3 * w + W_LORA:3 * w + W_LORA + A_LORA]
    gd = xs[:, 3 * w + W_LORA + A_LORA:]

    lw = -DECAY_SCALE * _sigmoid(w0_ref[...] + _mm(jnp.tanh(wd), wup_ref[...]))
    a = _sigmoid(a0_ref[...] + _mm(ad, aup_ref[...]))
    g = jnp.dot(_sigmoid(gd).astype(BF16), gup_ref[...].astype(BF16), preferred_element_type=F32)

    kk = k * kk_ref[...]
    ssq = _mm_exact_rhs(kk * kk, bd_ref[...])
    kn = kk / jnp.maximum(jnp.sqrt(ssq), 1e-12)
    r_o[0] = r
    lw_o[0] = lw
    k_o[0] = k * (1.0 + (a - 1.0) * ka_ref[...])
    v_o[0] = v
    kn_o[0] = kn
    b_o[0] = kn * a
    g_o[0] = g


def _rwkv_prep(h3, w_rw, mu, w0, w_up, a0, a_up, g_up, k_k, k_a, bd512):
    b, l, d = h3.shape
    nt = l // ROW_TILE
    out = jax.ShapeDtypeStruct((b, l, MIX_WIDTH), F32)
    row = pl.BlockSpec((1, ROW_TILE, MIX_WIDTH), lambda bi, t: (bi, t, 0))

    def full(arr):
        return pl.BlockSpec(arr.shape, lambda bi, t: (0,) * arr.ndim)

    consts = (w_rw, mu, w0, w_up, a0, a_up, g_up, k_k, k_a, bd512)
    return pl.pallas_call(
        _rwkv_prep_kernel,
        out_shape=(out,) * 7,
        grid=(b, nt),
        in_specs=[pl.BlockSpec((1, ROW_TILE, d), lambda bi, t: (bi, t, 0))] + [full(c) for c in consts],
        out_specs=(row,) * 7,
        scratch_shapes=[pltpu.VMEM((1, RW_WIDTH), F32)],
        compiler_params=_cparams(("parallel", "arbitrary")),
        name="rwkv_prep",
    )(h3, *consts)


def _rwkv_scan_kernel(r_ref, lw_ref, k_ref, v_ref, kn_ref, b_ref, g_ref,
                      rk_ref, lng_ref, lnb_ref, o_ref, state_sc):
    nb, rows, w = r_ref.shape
    c = SCAN_CHUNK
    nchunk = rows // c
    npair = w // PAIR
    assert 2 * c == PAIR

    @pl.when(pl.program_id(0) == 0)
    def _():
        state_sc[...] = jnp.zeros_like(state_sc)

    rr = lax.broadcasted_iota(jnp.int32, (c, c), 0)
    cc = lax.broadcasted_iota(jnp.int32, (c, c), 1)
    ones_incl = (rr >= cc).astype(BF16)
    row_c = lax.broadcasted_iota(jnp.int32, (c, PAIR), 0)
    key_c = lax.broadcasted_iota(jnp.int32, (c, PAIR), 1) & (HEAD_DIM - 1)
    strict_m = key_c < row_c
    incl_m = key_c <= row_c
    first = lax.broadcasted_iota(jnp.int32, (1, PAIR), 1) < HEAD_DIM
    pr = lax.broadcasted_iota(jnp.int32, (PAIR, PAIR), 0)
    pc = lax.broadcasted_iota(jnp.int32, (PAIR, PAIR), 1)
    same_head = (pr < HEAD_DIM) == (pc < HEAD_DIM)
    head0_rows = lax.broadcasted_iota(jnp.int32, (PAIR, 2 * PAIR), 0) < HEAD_DIM
    head_ones = same_head.astype(BF16)
    eye_p = pr == pc
    eye_f = eye_p.astype(F32)
    zeros_c = jnp.zeros((c, PAIR), F32)

    def cat0(*xs):
        return jnp.concatenate(xs, axis=0)

    def cat1(*xs):
        return jnp.concatenate(xs, axis=1)

    def mm(a, b, dims=_NN):
        return _dg(a.astype(BF16), b.astype(BF16), dims)

    streams = [(bi, p, ci) for bi in range(nb) for p in range(npair) for ci in range(nchunk)]

    def sl_of(s):
        bi, p, ci = s
        return (bi, slice(ci * c, (ci + 1) * c), slice(p * PAIR, (p + 1) * PAIR))

    pre = {}
    for bi in range(nb):
        for ci in range(nchunk):
            rs = (bi, slice(ci * c, (ci + 1) * c), slice(None))
            lw = lw_ref[rs]
            cl = _mm_exact_lhs(ones_incl, lw)
            cl_end = cl[c - 1:c, :]
            e_pos = jnp.exp(cl)
            e_neg = jnp.exp(-cl)
            e_end = jnp.exp(cl_end - cl)
            r, k, kn, b = r_ref[rs], k_ref[rs], kn_ref[rs], b_ref[rs]
            full = dict(a_t=-kn * jnp.exp(cl - lw), r_t=r * e_pos, b_t=b * e_neg, k_t=k * e_neg,
                        b_h=b * e_end, k_h=k * e_end, g_end=jnp.exp(cl_end))
            for p in range(npair):
                pre[(bi, p, ci)] = {n: x[:, p * PAIR:(p + 1) * PAIR] for n, x in full.items()}

    tri = {}
    for s in streams:
        d = pre[s]
        a_t, r_t, b_t, k_t = d["a_t"], d["r_t"], d["b_t"], d["k_t"]
        bk = cat0(b_t, k_t)
        s0 = mm(jnp.where(first, cat0(a_t, r_t), 0.0), bk, _NT)
        s1t = mm(jnp.where(first, 0.0, a_t), cat0(k_t, b_t), _NT)
        s1b = mm(jnp.where(first, 0.0, r_t), bk, _NT)
        top0 = jnp.where(strict_m, s0[:c], 0.0)
        top1 = jnp.where(strict_m, s1t, 0.0)
        tri[s] = dict(
            l_ab=cat0(jnp.where(first, top0, 0.0), jnp.where(first, 0.0, top1)),
            l_ak=cat0(jnp.where(first, 0.0, top0), jnp.where(first, top1, 0.0)),
            m0=jnp.where(incl_m, s0[c:], 0.0), m1=jnp.where(incl_m, s1b, 0.0))

    xs = {}
    for s in streams:
        v = v_ref[sl_of(s)]
        xs[s] = mm(tri[s]["l_ak"], cat0(v, v))

    tinv = {s: eye_f + tri[s]["l_ab"] for s in streams}
    pw = {s: tri[s]["l_ab"] for s in streams}
    n = 2
    while n < c:
        for s in streams:
            pw[s] = mm(pw[s], pw[s])
        for s in streams:
            tinv[s] = tinv[s] + mm(tinv[s], pw[s])
        n *= 2

    pq = {}
    for s in streams:
        a_t = pre[s]["a_t"]
        rhs = cat0(cat1(a_t, xs[s][:c]), cat1(a_t, xs[s][c:]))
        pq[s] = mm(tinv[s], rhs)

    coef = {}
    for s in streams:
        d = pre[s]
        v = v_ref[sl_of(s)]
        bkh_t = cat0(d["b_h"], d["k_h"]).T
        zv = cat1(zeros_c, v)
        res0 = mm(cat0(tri[s]["m0"], bkh_t), cat0(pq[s][:c], zv))
        res1 = mm(cat0(tri[s]["m1"], bkh_t), cat0(pq[s][c:], zv))
        ry = d["r_t"] + jnp.where(first, res0[:c, :PAIR], res1[:c, :PAIR])
        y0 = jnp.where(first, res0[:c, PAIR:], res1[:c, PAIR:])
        gj = jnp.where(head0_rows, res0[c:], res1[c:])
        gmat = jnp.where(same_head, gj[:, :PAIR], 0.0) + jnp.where(eye_p, d["g_end"], 0.0)
        jmat = jnp.where(same_head, gj[:, PAIR:], 0.0)
        coef[s] = (cat0(ry, gmat), y0, jmat)

    ys = {}
    for bi in range(nb):
        for p in range(npair):
            idx = bi * npair + p
            state = state_sc[idx]
            for ci in range(nchunk):
                lhs, y0, jmat = coef[(bi, p, ci)]
                st = mm(lhs, state)
                ys[(bi, p, ci)] = st[:c] + y0
                state = st[c:] + jmat
            state_sc[idx] = state

    for s in streams:
        sl = sl_of(s)
        csl = (slice(None), sl[2])
        y = ys[s]
        v = v_ref[sl]
        r, k = r_ref[sl], k_ref[sl]
        mean = _mm_exact_rhs(y, head_ones) * (1.0 / HEAD_DIM)
        yc = y - mean
        var = _mm_exact_rhs(yc * yc, head_ones) * (1.0 / HEAD_DIM)
        yn = yc * lax.rsqrt(var + LNX_EPS) * lng_ref[csl] + lnb_ref[csl]
        bonus = _mm_exact_rhs(r * k * rk_ref[csl], head_ones) * v
        o_ref[sl] = ((yn + bonus) * g_ref[sl]).astype(o_ref.dtype)


def _rwkv_scan(r, lw, k, v, kn, bb, g, r_k, lnx_g, lnx_b):
    b, l, w = r.shape
    c = SCAN_CHUNK * SCAN_CHUNKS_PER_STEP
    row = pl.BlockSpec((b, c, w), lambda i: (0, i, 0))
    vec = pl.BlockSpec((1, w), lambda i: (0, 0))
    return pl.pallas_call(
        _rwkv_scan_kernel,
        out_shape=jax.ShapeDtypeStruct((b, l, w), BF16),
        grid=(l // c,),
        in_specs=[row] * 7 + [vec] * 3,
        out_specs=row,
        scratch_shapes=[pltpu.VMEM((b * (w // PAIR), PAIR, PAIR), F32)],
        compiler_params=_cparams(("arbitrary",)),
        name="rwkv_scan",
    )(r, lw, k, v, kn, bb, g, r_k, lnx_g, lnx_b)


def _merge_kernel(h_ref, osb_ref, orw_ref, wg_ref, psb_ref, prw_ref, wout_ref,
                  g1_ref, b1_ref, rw_ref, rb_ref, h1_ref, comb_ref):
    h = h_ref[...]
    hb = h.astype(BF16)
    gates = _sigmoid(jnp.dot(hb, wg_ref[...], preferred_element_type=F32))
    sb = jnp.dot(osb_ref[...], psb_ref[...], preferred_element_type=F32)
    rw = jnp.dot(orw_ref[...], prw_ref[...], preferred_element_type=F32)
    merged = gates[:, :D_MODEL] * sb + gates[:, D_MODEL:] * rw
    mix = jnp.dot(merged.astype(BF16), wout_ref[...], preferred_element_type=F32)
    h1 = _layer_norm(ALPHA * h + mix, g1_ref[...], b1_ref[...])
    h1_ref[...] = h1

    logits = _mm(h1, rw_ref[...]) + rb_ref[...]
    m = jnp.max(logits, axis=-1, keepdims=True)
    ex = jnp.exp(logits - m)
    probs = ex / jnp.sum(ex, axis=-1, keepdims=True)
    lane = lax.broadcasted_iota(jnp.int32, probs.shape, 1)
    big = N_EXPERTS

    def top2(vals):
        v1 = jnp.max(vals, axis=-1, keepdims=True)
        i1 = jnp.min(jnp.where(vals == v1, lane, big), axis=-1, keepdims=True)
        rest = jnp.where(lane == i1, -2.0, vals)
        v2 = jnp.max(rest, axis=-1, keepdims=True)
        i2 = jnp.min(jnp.where(rest == v2, lane, big), axis=-1, keepdims=True)
        return v1, i1, v2, i2

    best = None
    sel = None
    for gi in range(N_GROUPS):
        in_g = (lane >= gi * GROUP) & (lane < (gi + 1) * GROUP)
        v1, _, v2, _ = top2(jnp.where(in_g, probs, -1.0))
        score = v1 + v2
        if gi == 0:
            best, sel = score, jnp.zeros_like(score, dtype=jnp.int32)
        else:
            better = score > best
            best = jnp.where(better, score, best)
            sel = jnp.where(better, gi, sel)
    in_sel = (lane >= sel * GROUP) & (lane < (sel + 1) * GROUP)
    v1, i1, v2, i2 = top2(jnp.where(in_sel, probs, -1.0))
    denom = v1 + v2
    comb_ref[...] = jnp.where(lane == i1, v1 / denom, 0.0) + jnp.where(lane == i2, v2 / denom, 0.0)


def _merge(h, o_sb, o_rw, w_gates, p_sb, p_rwkv, w_out, ln_g, ln_b, router_w, router_b):
    t, d = h.shape

    def full(arr):
        return pl.BlockSpec(arr.shape, lambda i: (0,) * arr.ndim)

    consts = (w_gates, p_sb, p_rwkv, w_out, ln_g, ln_b, router_w, router_b)
    return pl.pallas_call(
        _merge_kernel,
        out_shape=(jax.ShapeDtypeStruct((t, d), F32), jax.ShapeDtypeStruct((t, N_EXPERTS), F32)),
        grid=(t // ROW_TILE,),
        in_specs=[pl.BlockSpec((ROW_TILE, d), lambda i: (i, 0)),
                  pl.BlockSpec((ROW_TILE, MIX_WIDTH), lambda i: (i, 0)),
                  pl.BlockSpec((ROW_TILE, MIX_WIDTH), lambda i: (i, 0))] + [full(c) for c in consts],
        out_specs=(pl.BlockSpec((ROW_TILE, d), lambda i: (i, 0)),
                   pl.BlockSpec((ROW_TILE, N_EXPERTS), lambda i: (i, 0))),
        compiler_params=_cparams(("parallel",)),
        name="merge_ln1_router",
    )(h, o_sb, o_rw, *consts)


def _moe_kernel(h_ref, comb_ref, wg_ref, wu_ref, wd_ref, g2_ref, b2_ref, o_ref, acc_sc):
    e = pl.program_id(1)

    @pl.when(e == 0)
    def _():
        acc_sc[...] = jnp.zeros_like(acc_sc)

    x = h_ref[...].astype(BF16)
    gate = jnp.dot(x, wg_ref[0, 0].astype(BF16), preferred_element_type=F32)
    up = jnp.dot(x, wu_ref[0, 0].astype(BF16), preferred_element_type=F32)
    hid = gate * _sigmoid(gate) * up
    comb = comb_ref[...]
    lane = lax.broadcasted_iota(jnp.int32, comb.shape, 1)
    ce = jnp.sum(jnp.where(lane == e, comb, 0.0), axis=-1, keepdims=True)
    acc_sc[...] += ce * jnp.dot(hid.astype(BF16), wd_ref[0, 0].astype(BF16), preferred_element_type=F32)

    @pl.when(e == pl.num_programs(1) - 1)
    def _():
        o_ref[...] = _layer_norm(ALPHA * h_ref[...] + acc_sc[...], g2_ref[...], b2_ref[...])


def _moe(h, comb, wg, wu, wd, li, ln_g, ln_b):
    t, d = h.shape
    _, ne, _, f = wg.shape
    rows = MOE_ROW_TILE
    return pl.pallas_call(
        _moe_kernel,
        out_shape=jax.ShapeDtypeStruct((t, d), F32),
        grid=(t // rows, ne),
        in_specs=[pl.BlockSpec((rows, d), lambda i, e: (i, 0), pipeline_mode=pl.Buffered(1)),
                  pl.BlockSpec((rows, ne), lambda i, e: (i, 0)),
                  pl.BlockSpec((1, 1, d, f), lambda i, e: (li, e, 0, 0)),
                  pl.BlockSpec((1, 1, d, f), lambda i, e: (li, e, 0, 0)),
                  pl.BlockSpec((1, 1, f, d), lambda i, e: (li, e, 0, 0)),
                  pl.BlockSpec((1, d), lambda i, e: (0, 0)),
                  pl.BlockSpec((1, d), lambda i, e: (0, 0))],
        out_specs=pl.BlockSpec((rows, d), lambda i, e: (i, 0)),
        scratch_shapes=[pltpu.VMEM((rows, d), F32)],
        compiler_params=pltpu.CompilerParams(dimension_semantics=("parallel", "arbitrary"),
                                             vmem_limit_bytes=MOE_VMEM_LIMIT),
        name="moe_ln2",
    )(h, comb, wg, wu, wd, ln_g, ln_b)


def _layer(h, lp, shared, batch):
    t, d = h.shape
    l = t // batch
    q, k, v = _qkv_proj(h, lp["w_qkv"])
    shp = (batch, l, MIX_WIDTH)
    o_sb = _sb_attention(q.reshape(shp), k.reshape(shp), v.reshape(shp)).reshape(t, MIX_WIDTH)
    prep = _rwkv_prep(h.reshape(batch, l, d), lp["w_rw"], lp["mu"], lp["w0"], lp["w_up"], lp["a0"],
                      lp["a_up"], lp["g_up"], lp["k_k"], lp["k_a"], shared["bd512"])
    o_rw = _rwkv_scan(*prep, lp["r_k"], lp["lnx_g"], lp["lnx_b"]).reshape(t, MIX_WIDTH)
    h1, comb = _merge(h, o_sb, o_rw, lp["w_gates"], lp["p_sb"], lp["p_rwkv"], lp["w_out"],
                      lp["ln1_g"], lp["ln1_b"], shared["router_w"], shared["router_b"])
    return _moe(h1, comb, shared["wg"], shared["wu"], shared["wd"], lp["li"], lp["ln2_g"], lp["ln2_b"])


def kernel(x, meta, emb_ln_g, emb_ln_b, w_in, rwkv_mu, w0, w_up, a0, a_up, g_up, k_k, k_a, r_k, lnx_g, lnx_b, p_sb, p_rwkv, w_out, ln1_g, ln1_b, router_w, router_b, exp_w_gate, exp_w_up, exp_w_down, ln2_g, ln2_b):
    batch, seq, d = x.shape
    n_meta = meta.shape[0]
    l = seq + n_meta
    lpad = -(-l // SEQ_ALIGN) * SEQ_ALIGN
    depth = w_in.shape[0]

    meta_b = jnp.broadcast_to(meta.astype(x.dtype)[None], (batch, n_meta, d))
    hcat = jnp.concatenate([meta_b, x, jnp.zeros((batch, lpad - l, d), x.dtype)], axis=1)
    h = _embed_ln(hcat.reshape(batch * lpad, d), emb_ln_g.reshape(1, d), emb_ln_b.reshape(1, d))

    idx = jnp.arange(MIX_WIDTH) // HEAD_DIM
    shared = {
        "bd512": (idx[:, None] == idx[None, :]).astype(BF16),
        "router_w": router_w,
        "wg": exp_w_gate, "wu": exp_w_up, "wd": exp_w_down,
        "router_b": router_b.reshape(1, -1),
    }
    qkv_w = 3 * MIX_WIDTH
    for li in range(depth):
        row = lambda a: a[li].reshape(1, -1)
        lp = {
            "w_qkv": w_in[li, :, :qkv_w].astype(BF16),
            "w_rw": w_in[li, :, qkv_w:qkv_w + RW_WIDTH].astype(BF16),
            "w_gates": w_in[li, :, qkv_w + RW_WIDTH:].astype(BF16),
            "mu": row(rwkv_mu), "w0": row(w0), "a0": row(a0), "k_k": row(k_k), "k_a": row(k_a),
            "w_up": w_up[li], "a_up": a_up[li], "g_up": g_up[li],
            "r_k": row(r_k), "lnx_g": row(lnx_g), "lnx_b": row(lnx_b),
            "p_sb": p_sb[li].astype(BF16), "p_rwkv": p_rwkv[li].astype(BF16),
            "w_out": w_out[li].astype(BF16),
            "ln1_g": row(ln1_g), "ln1_b": row(ln1_b), "ln2_g": row(ln2_g), "ln2_b": row(ln2_b),
            "li": li,
        }
        h = _layer(h, lp, shared, batch)
    return h.reshape(batch, lpad, d)[:, n_meta:l]
```

```python
import functools
import math

import jax
import jax.numpy as jnp
from jax import lax
from jax.experimental import pallas as pl
from jax.experimental.pallas import tpu as pltpu

F32 = jnp.float32
BF16 = jnp.bfloat16

D_MODEL = 1024
DEPTH = 2
N_META = 16
HEAD_DIM = 64
PAIR = 2 * HEAD_DIM
MIX_WIDTH = 512
W_LORA, A_LORA, G_LORA = 64, 64, 128
RW_WIDTH = 3 * MIX_WIDTH + W_LORA + A_LORA + G_LORA
DECAY_SCALE = math.exp(-0.5)
LNX_EPS = 64e-5
LN_EPS = 1e-5
N_EXPERTS = 16
N_GROUPS = 4
GROUP = N_EXPERTS // N_GROUPS
D_EXPERT = 512
ALPHA = (2 * DEPTH) ** 0.25

SEQ_ALIGN = 640
ROW_TILE = 640
ATT_BLOCK = 128
ATT_PAIRS = 4
SCAN_CHUNK = 64
SCAN_CHUNKS_PER_STEP = 2
EXP_ZERO_BELOW = -104.0
MOE_ROW_TILE = 1280
MOE_CHUNK = 256
MOE_ALIGN = 16
VMEM_LIMIT = 48 * 1024 * 1024
MOE_VMEM_LIMIT = 58 * 1024 * 1024


def _cparams(sem):
    return pltpu.CompilerParams(dimension_semantics=sem, vmem_limit_bytes=VMEM_LIMIT)


def _split2(x):
    hi = x.astype(BF16)
    lo = (x - hi.astype(F32)).astype(BF16)
    return hi, lo


def _split3(x):
    hi = x.astype(BF16)
    r1 = x - hi.astype(F32)
    mid = r1.astype(BF16)
    lo = (r1 - mid.astype(F32)).astype(BF16)
    return hi, mid, lo


_NN = (((1,), (0,)), ((), ()))
_NT = (((1,), (1,)), ((), ()))


def _dg(a, b, dims):
    return lax.dot_general(a, b, dims, preferred_element_type=F32)


def _mm(a, b, dims=_NN):
    ah, al = _split2(a)
    bh, bl = _split2(b)
    return _dg(ah, bh, dims) + (_dg(ah, bl, dims) + _dg(al, bh, dims))


def _mm_exact_lhs(a_bf16, b, dims=_NN):
    b1, b2, b3 = _split3(b)
    return _dg(a_bf16, b1, dims) + (_dg(a_bf16, b2, dims) + _dg(a_bf16, b3, dims))


def _mm_exact_rhs(a, b_bf16, dims=_NN):
    a1, a2 = _split2(a)
    return _dg(a1, b_bf16, dims) + _dg(a2, b_bf16, dims)


def _sigmoid(x):
    return 1.0 / (1.0 + jnp.exp(-x))


def _layer_norm(x, g, b):
    mu = jnp.mean(x, axis=-1, keepdims=True)
    xc = x - mu
    var = jnp.mean(xc * xc, axis=-1, keepdims=True)
    return xc * lax.rsqrt(var + LN_EPS) * g + b


def _embed_ln_kernel(x_ref, g_ref, b_ref, o_ref):
    o_ref[...] = _layer_norm(x_ref[...], g_ref[...], b_ref[...])


def _embed_ln(x, g, b):
    t, d = x.shape
    return pl.pallas_call(
        _embed_ln_kernel,
        out_shape=jax.ShapeDtypeStruct((t, d), F32),
        grid=(t // ROW_TILE,),
        in_specs=[pl.BlockSpec((ROW_TILE, d), lambda i: (i, 0)),
                  pl.BlockSpec((1, d), lambda i: (0, 0)),
                  pl.BlockSpec((1, d), lambda i: (0, 0))],
        out_specs=pl.BlockSpec((ROW_TILE, d), lambda i: (i, 0)),
        compiler_params=_cparams(("parallel",)),
        name="embed_ln",
    )(x, g, b)


def _qkv_kernel(h_ref, w_ref, q_ref, k_ref, v_ref):
    x = h_ref[...].astype(BF16)
    p = jnp.dot(x, w_ref[...], preferred_element_type=F32)
    q_ref[...] = (p[:, :MIX_WIDTH] * (HEAD_DIM ** -0.5)).astype(BF16)
    k_ref[...] = p[:, MIX_WIDTH:2 * MIX_WIDTH].astype(BF16)
    v_ref[...] = p[:, 2 * MIX_WIDTH:].astype(BF16)


def _qkv_proj(h, w_qkv):
    t, d = h.shape
    out = jax.ShapeDtypeStruct((t, MIX_WIDTH), BF16)
    row = pl.BlockSpec((ROW_TILE, MIX_WIDTH), lambda i: (i, 0))
    return pl.pallas_call(
        _qkv_kernel,
        out_shape=(out, out, out),
        grid=(t // ROW_TILE,),
        in_specs=[pl.BlockSpec((ROW_TILE, d), lambda i: (i, 0)),
                  pl.BlockSpec((d, 3 * MIX_WIDTH), lambda i: (0, 0))],
        out_specs=(row, row, row),
        compiler_params=_cparams(("parallel",)),
        name="qkv_proj",
    )(h, w_qkv)


def _sb_attn_kernel(q_ref, k_ref, v_ref, o_ref, acc_sc):
    blk = ATT_BLOCK
    npair = q_ref.shape[2] // PAIR
    i = pl.program_id(2)
    first = lax.broadcasted_iota(jnp.int32, (1, PAIR), 1) < HEAD_DIM
    heads = []
    for p in range(npair):
        q = q_ref[0, :, p * PAIR:(p + 1) * PAIR]
        zero = jnp.zeros_like(q)
        heads += [(p, jnp.where(first, q, zero)), (p, jnp.where(first, zero, q))]
    rr = lax.broadcasted_iota(jnp.int32, (blk, blk), 0)
    cc = lax.broadcasted_iota(jnp.int32, (blk, blk), 1)
    causal = cc < rr
    later = (rr > cc).astype(BF16)

    def sweep(j, cs, diagonal):
        start = pl.multiple_of(j * blk, blk)
        kbs = [k_ref[0, pl.ds(start, blk), p * PAIR:(p + 1) * PAIR] for p in range(npair)]
        vbs = [v_ref[0, pl.ds(start, blk), p * PAIR:(p + 1) * PAIR] for p in range(npair)]
        zs = [_dg(qh, kbs[p], _NT) for p, qh in heads]
        sps = [jnp.maximum(z, 0.0) + jnp.log(1.0 + jnp.exp(-jnp.abs(z))) for z in zs]
        us = [jnp.where(causal, sp, 0.0) for sp in sps] if diagonal else sps
        rests = [c - _mm_exact_rhs(u, later) for u, c in zip(us, cs)]
        ws = [jnp.exp((z - sp) + rest) for z, sp, rest in zip(zs, sps, rests)]
        if diagonal:
            ws = [jnp.where(causal, a, 0.0) for a in ws]
        pvs = [jnp.dot(a.astype(BF16), vbs[p], preferred_element_type=F32) for (p, _), a in zip(heads, ws)]
        for p in range(npair):
            pv = jnp.where(first, pvs[2 * p], pvs[2 * p + 1])
            if diagonal:
                acc_sc[:, p * PAIR:(p + 1) * PAIR] = pv
            else:
                acc_sc[:, p * PAIR:(p + 1) * PAIR] += pv
        return tuple(c - jnp.sum(u, axis=1, keepdims=True) for c, u in zip(cs, us))

    def live(cs):
        m = cs[0]
        for c in cs[1:]:
            m = jnp.maximum(m, c)
        return jnp.max(m) > EXP_ZERO_BELOW

    cs = sweep(i, (jnp.zeros((blk, 1), F32),) * len(heads), True)

    def cond(st):
        j, _, alive = st
        return jnp.logical_and(j >= 0, alive)

    def body(st):
        j, cs, _ = st
        cs = sweep(j, cs, False)
        return j - 1, cs, live(cs)

    lax.while_loop(cond, body, (i - 1, cs, live(cs)))
    o_ref[0] = acc_sc[...].astype(o_ref.dtype)


def _sb_attention(q, k, v):
    b, l, w = q.shape
    wb = ATT_PAIRS * PAIR
    return pl.pallas_call(
        _sb_attn_kernel,
        out_shape=jax.ShapeDtypeStruct((b, l, w), BF16),
        grid=(b, w // wb, l // ATT_BLOCK),
        in_specs=[pl.BlockSpec((1, ATT_BLOCK, wb), lambda bi, p, i: (bi, i, p)),
                  pl.BlockSpec((1, l, wb), lambda bi, p, i: (bi, 0, p)),
                  pl.BlockSpec((1, l, wb), lambda bi, p, i: (bi, 0, p))],
        out_specs=pl.BlockSpec((1, ATT_BLOCK, wb), lambda bi, p, i: (bi, i, p)),
        scratch_shapes=[pltpu.VMEM((ATT_BLOCK, wb), F32)],
        compiler_params=_cparams(("parallel", "parallel", "arbitrary")),
        name="sb_attention",
    )(q, k, v)


def _rwkv_prep_kernel(h_ref, w_ref, mu_ref, w0_ref, wup_ref, a0_ref, aup_ref, gup_ref,
                      kk_ref, ka_ref, bd_ref,
                      r_o, lw_o, k_o, v_o, kn_o, b_o, g_o, prev_sc):
    t = pl.program_id(1)
    x = h_ref[0].astype(BF16)
    p = jnp.dot(x, w_ref[...], preferred_element_type=F32)
    rows = p.shape[0]

    @pl.when(t == 0)
    def _():
        prev_sc[...] = jnp.zeros_like(prev_sc)

    row = lax.broadcasted_iota(jnp.int32, (rows, 1), 0)
    shifted = jnp.where(row == 0, prev_sc[...], pltpu.roll(p, 1, axis=0))
    prev_sc[...] = p[rows - 1:rows, :]
    xs = p + (shifted - p) * mu_ref[...]

    w = MIX_WIDTH
    r, k, v = xs[:, :w], xs[:, w:2 * w], xs[:, 2 * w:3 * w]
    wd = xs[:, 3 * w:3 * w + W_LORA]
    ad = xs[:, 3 * w + W_LORA:3 * w + W_LORA + A_LORA]
    gd = xs[:, 3 * w + W_LORA + A_LORA:]

    lw = -DECAY_SCALE * _sigmoid(w0_ref[...] + _mm(jnp.tanh(wd), wup_ref[...]))
    a = _sigmoid(a0_ref[...] + _mm(ad, aup_ref[...]))
    g = jnp.dot(_sigmoid(gd).astype(BF16), gup_ref[...].astype(BF16), preferred_element_type=F32)

    kk = k * kk_ref[...]
    ssq = _mm_exact_rhs(kk * kk, bd_ref[...])
    kn = kk / jnp.maximum(jnp.sqrt(ssq), 1e-12)
    r_o[0] = r
    lw_o[0] = lw
    k_o[0] = k * (1.0 + (a - 1.0) * ka_ref[...])
    v_o[0] = v
    kn_o[0] = kn
    b_o[0] = kn * a
    g_o[0] = g


def _rwkv_prep(h3, w_rw, mu, w0, w_up, a0, a_up, g_up, k_k, k_a, bd512):
    b, l, d = h3.shape
    nt = l // ROW_TILE
    out = jax.ShapeDtypeStruct((b, l, MIX_WIDTH), F32)
    row = pl.BlockSpec((1, ROW_TILE, MIX_WIDTH), lambda bi, t: (bi, t, 0))

    def full(arr):
        return pl.BlockSpec(arr.shape, lambda bi, t: (0,) * arr.ndim)

    consts = (w_rw, mu, w0, w_up, a0, a_up, g_up, k_k, k_a, bd512)
    return pl.pallas_call(
        _rwkv_prep_kernel,
        out_shape=(out,) * 7,
        grid=(b, nt),
        in_specs=[pl.BlockSpec((1, ROW_TILE, d), lambda bi, t: (bi, t, 0))] + [full(c) for c in consts],
        out_specs=(row,) * 7,
        scratch_shapes=[pltpu.VMEM((1, RW_WIDTH), F32)],
        compiler_params=_cparams(("parallel", "arbitrary")),
        name="rwkv_prep",
    )(h3, *consts)


def _rwkv_scan_kernel(r_ref, lw_ref, k_ref, v_ref, kn_ref, b_ref, g_ref,
                      rk_ref, lng_ref, lnb_ref, o_ref, state_sc):
    nb, rows, w = r_ref.shape
    c = SCAN_CHUNK
    nchunk = rows // c
    npair = w // PAIR
    assert 2 * c == PAIR

    @pl.when(pl.program_id(0) == 0)
    def _():
        state_sc[...] = jnp.zeros_like(state_sc)

    rr = lax.broadcasted_iota(jnp.int32, (c, c), 0)
    cc = lax.broadcasted_iota(jnp.int32, (c, c), 1)
    ones_incl = (rr >= cc).astype(BF16)
    row_c = lax.broadcasted_iota(jnp.int32, (c, PAIR), 0)
    key_c = lax.broadcasted_iota(jnp.int32, (c, PAIR), 1) & (HEAD_DIM - 1)
    strict_m = key_c < row_c
    incl_m = key_c <= row_c
    first = lax.broadcasted_iota(jnp.int32, (1, PAIR), 1) < HEAD_DIM
    pr = lax.broadcasted_iota(jnp.int32, (PAIR, PAIR), 0)
    pc = lax.broadcasted_iota(jnp.int32, (PAIR, PAIR), 1)
    same_head = (pr < HEAD_DIM) == (pc < HEAD_DIM)
    head0_rows = lax.broadcasted_iota(jnp.int32, (PAIR, 2 * PAIR), 0) < HEAD_DIM
    head_ones = same_head.astype(BF16)
    eye_p = pr == pc
    eye_f = eye_p.astype(F32)
    zeros_c = jnp.zeros((c, PAIR), F32)

    def cat0(*xs):
        return jnp.concatenate(xs, axis=0)

    def cat1(*xs):
        return jnp.concatenate(xs, axis=1)

    def mm(a, b, dims=_NN):
        return _dg(a.astype(BF16), b.astype(BF16), dims)

    streams = [(bi, p, ci) for bi in range(nb) for p in range(npair) for ci in range(nchunk)]

    def sl_of(s):
        bi, p, ci = s
        return (bi, slice(ci * c, (ci + 1) * c), slice(p * PAIR, (p + 1) * PAIR))

    pre = {}
    for bi in range(nb):
        for ci in range(nchunk):
            rs = (bi, slice(ci * c, (ci + 1) * c), slice(None))
            lw = lw_ref[rs]
            cl = _mm_exact_lhs(ones_incl, lw)
            cl_end = cl[c - 1:c, :]
            e_pos = jnp.exp(cl)
            e_neg = jnp.exp(-cl)
            e_end = jnp.exp(cl_end - cl)
            r, k, kn, b = r_ref[rs], k_ref[rs], kn_ref[rs], b_ref[rs]
            full = dict(a_t=-kn * jnp.exp(cl - lw), r_t=r * e_pos, b_t=b * e_neg, k_t=k * e_neg,
                        b_h=b * e_end, k_h=k * e_end, g_end=jnp.exp(cl_end))
            for p in range(npair):
                pre[(bi, p, ci)] = {n: x[:, p * PAIR:(p + 1) * PAIR] for n, x in full.items()}

    tri = {}
    for s in streams:
        d = pre[s]
        a_t, r_t, b_t, k_t = d["a_t"], d["r_t"], d["b_t"], d["k_t"]
        bk = cat0(b_t, k_t)
        s0 = mm(jnp.where(first, cat0(a_t, r_t), 0.0), bk, _NT)
        s1t = mm(jnp.where(first, 0.0, a_t), cat0(k_t, b_t), _NT)
        s1b = mm(jnp.where(first, 0.0, r_t), bk, _NT)
        top0 = jnp.where(strict_m, s0[:c], 0.0)
        top1 = jnp.where(strict_m, s1t, 0.0)
        tri[s] = dict(
            l_ab=cat0(jnp.where(first, top0, 0.0), jnp.where(first, 0.0, top1)),
            l_ak=cat0(jnp.where(first, 0.0, top0), jnp.where(first, top1, 0.0)),
            m0=jnp.where(incl_m, s0[c:], 0.0), m1=jnp.where(incl_m, s1b, 0.0))

    xs = {}
    for s in streams:
        v = v_ref[sl_of(s)]
        xs[s] = mm(tri[s]["l_ak"], cat0(v, v))

    tinv = {s: eye_f + tri[s]["l_ab"] for s in streams}
    pw = {s: tri[s]["l_ab"] for s in streams}
    n = 2
    while n < c:
        for s in streams:
            pw[s] = mm(pw[s], pw[s])
        for s in streams:
            tinv[s] = tinv[s] + mm(tinv[s], pw[s])
        n *= 2

    pq = {}
    for s in streams:
        a_t = pre[s]["a_t"]
        rhs = cat0(cat1(a_t, xs[s][:c]), cat1(a_t, xs[s][c:]))
        pq[s] = mm(tinv[s], rhs)

    coef = {}
    for s in streams:
        d = pre[s]
        v = v_ref[sl_of(s)]
        bkh_t = cat0(d["b_h"], d["k_h"]).T
        zv = cat1(zeros_c, v)
        res0 = mm(cat0(tri[s]["m0"], bkh_t), cat0(pq[s][:c], zv))
        res1 = mm(cat0(tri[s]["m1"], bkh_t), cat0(pq[s][c:], zv))
        ry = d["r_t"] + jnp.where(first, res0[:c, :PAIR], res1[:c, :PAIR])
        y0 = jnp.where(first, res0[:c, PAIR:], res1[:c, PAIR:])
        gj = jnp.where(head0_rows, res0[c:], res1[c:])
        gmat = jnp.where(same_head, gj[:, :PAIR], 0.0) + jnp.where(eye_p, d["g_end"], 0.0)
        jmat = jnp.where(same_head, gj[:, PAIR:], 0.0)
        coef[s] = (cat0(ry, gmat), y0, jmat)

    ys = {}
    for bi in range(nb):
        for p in range(npair):
            idx = bi * npair + p
            state = state_sc[idx]
            for ci in range(nchunk):
                lhs, y0, jmat = coef[(bi, p, ci)]
                st = mm(lhs, state)
                ys[(bi, p, ci)] = st[:c] + y0
                state = st[c:] + jmat
            state_sc[idx] = state

    for s in streams:
        sl = sl_of(s)
        csl = (slice(None), sl[2])
        y = ys[s]
        v = v_ref[sl]
        r, k = r_ref[sl], k_ref[sl]
        mean = _mm_exact_rhs(y, head_ones) * (1.0 / HEAD_DIM)
        yc = y - mean
        var = _mm_exact_rhs(yc * yc, head_ones) * (1.0 / HEAD_DIM)
        yn = yc * lax.rsqrt(var + LNX_EPS) * lng_ref[csl] + lnb_ref[csl]
        bonus = _mm_exact_rhs(r * k * rk_ref[csl], head_ones) * v
        o_ref[sl] = ((yn + bonus) * g_ref[sl]).astype(o_ref.dtype)


def _rwkv_scan(r, lw, k, v, kn, bb, g, r_k, lnx_g, lnx_b):
    b, l, w = r.shape
    c = SCAN_CHUNK * SCAN_CHUNKS_PER_STEP
    row = pl.BlockSpec((b, c, w), lambda i: (0, i, 0))
    vec = pl.BlockSpec((1, w), lambda i: (0, 0))
    return pl.pallas_call(
        _rwkv_scan_kernel,
        out_shape=jax.ShapeDtypeStruct((b, l, w), BF16),
        grid=(l // c,),
        in_specs=[row] * 7 + [vec] * 3,
        out_specs=row,
        scratch_shapes=[pltpu.VMEM((b * (w // PAIR), PAIR, PAIR), F32)],
        compiler_params=_cparams(("arbitrary",)),
        name="rwkv_scan",
    )(r, lw, k, v, kn, bb, g, r_k, lnx_g, lnx_b)


def _merge_kernel(h_ref, osb_ref, orw_ref, wg_ref, psb_ref, prw_ref, wout_ref,
                  g1_ref, b1_ref, rw_ref, rb_ref, h1_ref, comb_ref):
    h = h_ref[...]
    hb = h.astype(BF16)
    gates = _sigmoid(jnp.dot(hb, wg_ref[...], preferred_element_type=F32))
    sb = jnp.dot(osb_ref[...], psb_ref[...], preferred_element_type=F32)
    rw = jnp.dot(orw_ref[...], prw_ref[...], preferred_element_type=F32)
    merged = gates[:, :D_MODEL] * sb + gates[:, D_MODEL:] * rw
    mix = jnp.dot(merged.astype(BF16), wout_ref[...], preferred_element_type=F32)
    h1 = _layer_norm(ALPHA * h + mix, g1_ref[...], b1_ref[...])
    h1_ref[...] = h1

    logits = _mm(h1, rw_ref[...]) + rb_ref[...]
    m = jnp.max(logits, axis=-1, keepdims=True)
    ex = jnp.exp(logits - m)
    probs = ex / jnp.sum(ex, axis=-1, keepdims=True)
    lane = lax.broadcasted_iota(jnp.int32, probs.shape, 1)
    big = N_EXPERTS

    def top2(vals):
        v1 = jnp.max(vals, axis=-1, keepdims=True)
        i1 = jnp.min(jnp.where(vals == v1, lane, big), axis=-1, keepdims=True)
        rest = jnp.where(lane == i1, -2.0, vals)
        v2 = jnp.max(rest, axis=-1, keepdims=True)
        i2 = jnp.min(jnp.where(rest == v2, lane, big), axis=-1, keepdims=True)
        return v1, i1, v2, i2

    best = None
    sel = None
    for gi in range(N_GROUPS):
        in_g = (lane >= gi * GROUP) & (lane < (gi + 1) * GROUP)
        v1, _, v2, _ = top2(jnp.where(in_g, probs, -1.0))
        score = v1 + v2
        if gi == 0:
            best, sel = score, jnp.zeros_like(score, dtype=jnp.int32)
        else:
            better = score > best
            best = jnp.where(better, score, best)
            sel = jnp.where(better, gi, sel)
    in_sel = (lane >= sel * GROUP) & (lane < (sel + 1) * GROUP)
    v1, i1, v2, i2 = top2(jnp.where(in_sel, probs, -1.0))
    denom = v1 + v2
    comb_ref[...] = jnp.where(lane == i1, v1 / denom, 0.0) + jnp.where(lane == i2, v2 / denom, 0.0)


def _merge(h, o_sb, o_rw, w_gates, p_sb, p_rwkv, w_out, ln_g, ln_b, router_w, router_b):
    t, d = h.shape

    def full(arr):
        return pl.BlockSpec(arr.shape, lambda i: (0,) * arr.ndim)

    consts = (w_gates, p_sb, p_rwkv, w_out, ln_g, ln_b, router_w, router_b)
    return pl.pallas_call(
        _merge_kernel,
        out_shape=(jax.ShapeDtypeStruct((t, d), F32), jax.ShapeDtypeStruct((t, N_EXPERTS), F32)),
        grid=(t // ROW_TILE,),
        in_specs=[pl.BlockSpec((ROW_TILE, d), lambda i: (i, 0)),
                  pl.BlockSpec((ROW_TILE, MIX_WIDTH), lambda i: (i, 0)),
                  pl.BlockSpec((ROW_TILE, MIX_WIDTH), lambda i: (i, 0))] + [full(c) for c in consts],
        out_specs=(pl.BlockSpec((ROW_TILE, d), lambda i: (i, 0)),
                   pl.BlockSpec((ROW_TILE, N_EXPERTS), lambda i: (i, 0))),
        compiler_params=_cparams(("parallel",)),
        name="merge_ln1_router",
    )(h, o_sb, o_rw, *consts)


def _moe_kernel(h_ref, comb_ref, wg_ref, wu_ref, wd_ref, g2_ref, b2_ref, o_ref,
                xs_sc, ys_sc, col_sc, seg_sm):
    e = pl.program_id(1)
    tm, d = h_ref.shape
    ne = comb_ref.shape[1]
    blk = MOE_CHUNK
    ns = xs_sc.shape[0] - blk

    @pl.when(e == 0)
    def _dispatch():
        comb = comb_ref[...]
        sel = comb != 0.0
        self32 = jnp.where(sel, 1.0, 0.0)
        selb = self32.astype(BF16)
        lane = lax.broadcasted_iota(jnp.int32, (tm, ne), 1)
        ranks = []
        for r0 in range(0, tm, blk):
            rr = lax.broadcasted_iota(jnp.int32, (blk, tm), 0) + r0
            cc = lax.broadcasted_iota(jnp.int32, (blk, tm), 1)
            ranks.append(jnp.dot(jnp.where(cc < rr, 1.0, 0.0).astype(BF16), selb, preferred_element_type=F32))
        rank = jnp.concatenate(ranks, axis=0)
        counts = jnp.sum(self32, axis=0, keepdims=True).astype(jnp.int32)
        lane1 = lax.broadcasted_iota(jnp.int32, (1, ne), 1)
        off = jnp.int32(0)
        offv = jnp.zeros((1, ne), F32)
        for ex in range(ne):
            n = counts[0, ex]
            seg_sm[0, ex] = off
            seg_sm[1, ex] = n
            offv = jnp.where(lane1 == ex, off.astype(F32), offv)
            off = off + ((n + (MOE_ALIGN - 1)) // MOE_ALIGN) * MOE_ALIGN
        pos = offv + rank

        def pick(idx, arr):
            return jnp.sum(jnp.where(lane == idx, arr, 0.0), axis=1, keepdims=True)

        i_lo = jnp.min(jnp.where(sel, lane, ne), axis=1, keepdims=True)
        i_hi = jnp.max(jnp.where(sel, lane, -1), axis=1, keepdims=True)
        any_sel = i_hi >= 0
        two = i_hi != i_lo
        p_lo = jnp.where(any_sel, pick(i_lo, pos), -1.0)
        p_hi = jnp.where(two, pick(i_hi, pos), -1.0)
        g_lo = pick(i_lo, comb)
        g_hi = jnp.where(two, pick(i_hi, comb), 0.0)
        l128 = lax.broadcasted_iota(jnp.int32, (tm, 128), 1)
        cols = jnp.where(l128 == 0, p_lo, jnp.where(l128 == 1, p_hi, jnp.where(l128 == 2, g_lo,
                         jnp.where(l128 == 3, g_hi, 0.0))))
        col_sc[...] = cols
        rows_t = cols.T
        p_lo_r, p_hi_r = rows_t[0:1, :], rows_t[1:2, :]
        x = h_ref[...].astype(BF16)
        for r0 in range(0, ns, blk):
            ri = (lax.broadcasted_iota(jnp.int32, (blk, tm), 0) + r0).astype(F32)
            onehot = jnp.where((ri == p_lo_r) | (ri == p_hi_r), 1.0, 0.0).astype(BF16)
            xs_sc[r0:r0 + blk, :] = jnp.dot(onehot, x, preferred_element_type=F32).astype(BF16)
        xs_sc[ns:, :] = jnp.zeros((blk, d), BF16)
        ys_sc[...] = jnp.zeros_like(ys_sc)

    seg_off = seg_sm[0, e]
    seg_n = seg_sm[1, e]
    wg = wg_ref[0, 0].astype(BF16)
    wu = wu_ref[0, 0].astype(BF16)
    wd = wd_ref[0, 0].astype(BF16)

    def ffn_chunk(c, carry):
        start = pl.multiple_of(seg_off + c * blk, MOE_ALIGN)
        x = xs_sc[pl.ds(start, blk), :]
        gate = jnp.dot(x, wg, preferred_element_type=F32)
        up = jnp.dot(x, wu, preferred_element_type=F32)
        hid = gate * _sigmoid(gate) * up
        ys_sc[pl.ds(start, blk), :] = jnp.dot(hid.astype(BF16), wd, preferred_element_type=F32).astype(BF16)
        return carry

    lax.fori_loop(0, (seg_n + (blk - 1)) // blk, ffn_chunk, 0)

    @pl.when(e == pl.num_programs(1) - 1)
    def _combine():
        ys = ys_sc[0:ns, :]
        for r0 in range(0, tm, blk):
            cols = col_sc[r0:r0 + blk, :]
            p_lo, p_hi, g_lo, g_hi = cols[:, 0:1], cols[:, 1:2], cols[:, 2:3], cols[:, 3:4]
            li = lax.broadcasted_iota(jnp.int32, (blk, ns), 1).astype(F32)
            wsel = jnp.where(li == p_lo, g_lo, 0.0) + jnp.where(li == p_hi, g_hi, 0.0)
            ff = jnp.dot(wsel.astype(BF16), ys, preferred_element_type=F32)
            o_ref[r0:r0 + blk, :] = _layer_norm(ALPHA * h_ref[r0:r0 + blk, :] + ff, g2_ref[...], b2_ref[...])


def _moe(h, comb, wg, wu, wd, li, ln_g, ln_b):
    t, d = h.shape
    _, ne, _, f = wg.shape
    rows = MOE_ROW_TILE
    ns = -(-(2 * rows + ne * (MOE_ALIGN - 1)) // MOE_CHUNK) * MOE_CHUNK
    return pl.pallas_call(
        _moe_kernel,
        out_shape=jax.ShapeDtypeStruct((t, d), F32),
        grid=(t // rows, ne),
        in_specs=[pl.BlockSpec((rows, d), lambda i, e: (i, 0), pipeline_mode=pl.Buffered(1)),
                  pl.BlockSpec((rows, ne), lambda i, e: (i, 0)),
                  pl.BlockSpec((1, 1, d, f), lambda i, e: (li, e, 0, 0)),
                  pl.BlockSpec((1, 1, d, f), lambda i, e: (li, e, 0, 0)),
                  pl.BlockSpec((1, 1, f, d), lambda i, e: (li, e, 0, 0)),
                  pl.BlockSpec((1, d), lambda i, e: (0, 0)),
                  pl.BlockSpec((1, d), lambda i, e: (0, 0))],
        out_specs=pl.BlockSpec((rows, d), lambda i, e: (i, 0)),
        scratch_shapes=[pltpu.VMEM((ns + MOE_CHUNK, d), BF16),
                        pltpu.VMEM((ns + MOE_CHUNK, d), BF16),
                        pltpu.VMEM((rows, 128), F32),
                        pltpu.SMEM((2, ne), jnp.int32)],
        compiler_params=pltpu.CompilerParams(dimension_semantics=("parallel", "arbitrary"),
                                             vmem_limit_bytes=MOE_VMEM_LIMIT),
        name="moe_ln2",
    )(h, comb, wg, wu, wd, ln_g, ln_b)


def _layer(h, lp, shared, batch):
    t, d = h.shape
    l = t // batch
    q, k, v = _qkv_proj(h, lp["w_qkv"])
    shp = (batch, l, MIX_WIDTH)
    o_sb = _sb_attention(q.reshape(shp), k.reshape(shp), v.reshape(shp)).reshape(t, MIX_WIDTH)
    prep = _rwkv_prep(h.reshape(batch, l, d), lp["w_rw"], lp["mu"], lp["w0"], lp["w_up"], lp["a0"],
                      lp["a_up"], lp["g_up"], lp["k_k"], lp["k_a"], shared["bd512"])
    o_rw = _rwkv_scan(*prep, lp["r_k"], lp["lnx_g"], lp["lnx_b"]).reshape(t, MIX_WIDTH)
    h1, comb = _merge(h, o_sb, o_rw, lp["w_gates"], lp["p_sb"], lp["p_rwkv"], lp["w_out"],
                      lp["ln1_g"], lp["ln1_b"], shared["router_w"], shared["router_b"])
    return _moe(h1, comb, shared["wg"], shared["wu"], shared["wd"], lp["li"], lp["ln2_g"], lp["ln2_b"])


def kernel(x, meta, emb_ln_g, emb_ln_b, w_in, rwkv_mu, w0, w_up, a0, a_up, g_up, k_k, k_a, r_k, lnx_g, lnx_b, p_sb, p_rwkv, w_out, ln1_g, ln1_b, router_w, router_b, exp_w_gate, exp_w_up, exp_w_down, ln2_g, ln2_b):
    batch, seq, d = x.shape
    n_meta = meta.shape[0]
    l = seq + n_meta
    lpad = -(-l // SEQ_ALIGN) * SEQ_ALIGN
    depth = w_in.shape[0]

    meta_b = jnp.broadcast_to(meta.astype(x.dtype)[None], (batch, n_meta, d))
    hcat = jnp.concatenate([meta_b, x, jnp.zeros((batch, lpad - l, d), x.dtype)], axis=1)
    h = _embed_ln(hcat.reshape(batch * lpad, d), emb_ln_g.reshape(1, d), emb_ln_b.reshape(1, d))

    idx = jnp.arange(MIX_WIDTH) // HEAD_DIM
    shared = {
        "bd512": (idx[:, None] == idx[None, :]).astype(BF16),
        "router_w": router_w,
        "wg": exp_w_gate, "wu": exp_w_up, "wd": exp_w_down,
        "router_b": router_b.reshape(1, -1),
    }
    qkv_w = 3 * MIX_WIDTH
    for li in range(depth):
        row = lambda a: a[li].reshape(1, -1)
        lp = {
            "w_qkv": w_in[li, :, :qkv_w].astype(BF16),
            "w_rw": w_in[li, :, qkv_w:qkv_w + RW_WIDTH].astype(BF16),
            "w_gates": w_in[li, :, qkv_w + RW_WIDTH:].astype(BF16),
            "mu": row(rwkv_mu), "w0": row(w0), "a0": row(a0), "k_k": row(k_k), "k_a": row(k_a),
            "w_up": w_up[li], "a_up": a_up[li], "g_up": g_up[li],
            "r_k": row(r_k), "lnx_g": row(lnx_g), "lnx_b": row(lnx_b),
            "p_sb": p_sb[li].astype(BF16), "p_rwkv": p_rwkv[li].astype(BF16),
            "w_out": w_out[li].astype(BF16),
            "ln1_g": row(ln1_g), "ln1_b": row(ln1_b), "ln2_g": row(ln2_g), "ln2_b": row(ln2_b),
            "li": li,
        }
        h = _layer(h, lp, shared, batch)
    return h.reshape(batch, lpad, d)[:, n_meta:l]
```

```python
import functools
import math

import jax
import jax.numpy as jnp
from jax import lax
from jax.experimental import pallas as pl
from jax.experimental.pallas import tpu as pltpu

F32 = jnp.float32
BF16 = jnp.bfloat16

D_MODEL = 1024
DEPTH = 2
N_META = 16
HEAD_DIM = 64
PAIR = 2 * HEAD_DIM
MIX_WIDTH = 512
W_LORA, A_LORA, G_LORA = 64, 64, 128
RW_WIDTH = 3 * MIX_WIDTH + W_LORA + A_LORA + G_LORA
DECAY_SCALE = math.exp(-0.5)
LNX_EPS = 64e-5
LN_EPS = 1e-5
N_EXPERTS = 16
N_GROUPS = 4
GROUP = N_EXPERTS // N_GROUPS
D_EXPERT = 512
ALPHA = (2 * DEPTH) ** 0.25

SEQ_ALIGN = 640
ROW_TILE = 640
ATT_BLOCK = 128
ATT_PAIRS = 4
SCAN_CHUNK = 64
SCAN_CHUNKS_PER_STEP = 2
EXP_ZERO_BELOW = -104.0
MOE_ROW_TILE = 1280
MOE_CHUNK = 192
MOE_SORT_BLOCK = 256
MOE_ALIGN = 16
VMEM_LIMIT = 48 * 1024 * 1024
MOE_VMEM_LIMIT = 58 * 1024 * 1024


def _cparams(sem):
    return pltpu.CompilerParams(dimension_semantics=sem, vmem_limit_bytes=VMEM_LIMIT)


def _split2(x):
    hi = x.astype(BF16)
    lo = (x - hi.astype(F32)).astype(BF16)
    return hi, lo


def _split3(x):
    hi = x.astype(BF16)
    r1 = x - hi.astype(F32)
    mid = r1.astype(BF16)
    lo = (r1 - mid.astype(F32)).astype(BF16)
    return hi, mid, lo


_NN = (((1,), (0,)), ((), ()))
_NT = (((1,), (1,)), ((), ()))


def _dg(a, b, dims):
    return lax.dot_general(a, b, dims, preferred_element_type=F32)


def _mm(a, b, dims=_NN):
    ah, al = _split2(a)
    bh, bl = _split2(b)
    return _dg(ah, bh, dims) + (_dg(ah, bl, dims) + _dg(al, bh, dims))


def _mm_exact_lhs(a_bf16, b, dims=_NN):
    b1, b2, b3 = _split3(b)
    return _dg(a_bf16, b1, dims) + (_dg(a_bf16, b2, dims) + _dg(a_bf16, b3, dims))


def _mm_exact_rhs(a, b_bf16, dims=_NN):
    a1, a2 = _split2(a)
    return _dg(a1, b_bf16, dims) + _dg(a2, b_bf16, dims)


def _sigmoid(x):
    return 1.0 / (1.0 + jnp.exp(-x))


def _layer_norm(x, g, b):
    mu = jnp.mean(x, axis=-1, keepdims=True)
    xc = x - mu
    var = jnp.mean(xc * xc, axis=-1, keepdims=True)
    return xc * lax.rsqrt(var + LN_EPS) * g + b


def _embed_ln_kernel(x_ref, g_ref, b_ref, o_ref):
    o_ref[...] = _layer_norm(x_ref[...], g_ref[...], b_ref[...])


def _embed_ln(x, g, b):
    t, d = x.shape
    return pl.pallas_call(
        _embed_ln_kernel,
        out_shape=jax.ShapeDtypeStruct((t, d), F32),
        grid=(t // ROW_TILE,),
        in_specs=[pl.BlockSpec((ROW_TILE, d), lambda i: (i, 0)),
                  pl.BlockSpec((1, d), lambda i: (0, 0)),
                  pl.BlockSpec((1, d), lambda i: (0, 0))],
        out_specs=pl.BlockSpec((ROW_TILE, d), lambda i: (i, 0)),
        compiler_params=_cparams(("parallel",)),
        name="embed_ln",
    )(x, g, b)


def _qkv_kernel(h_ref, w_ref, q_ref, k_ref, v_ref):
    x = h_ref[...].astype(BF16)
    p = jnp.dot(x, w_ref[...], preferred_element_type=F32)
    q_ref[...] = (p[:, :MIX_WIDTH] * (HEAD_DIM ** -0.5)).astype(BF16)
    k_ref[...] = p[:, MIX_WIDTH:2 * MIX_WIDTH].astype(BF16)
    v_ref[...] = p[:, 2 * MIX_WIDTH:].astype(BF16)


def _qkv_proj(h, w_qkv):
    t, d = h.shape
    out = jax.ShapeDtypeStruct((t, MIX_WIDTH), BF16)
    row = pl.BlockSpec((ROW_TILE, MIX_WIDTH), lambda i: (i, 0))
    return pl.pallas_call(
        _qkv_kernel,
        out_shape=(out, out, out),
        grid=(t // ROW_TILE,),
        in_specs=[pl.BlockSpec((ROW_TILE, d), lambda i: (i, 0)),
                  pl.BlockSpec((d, 3 * MIX_WIDTH), lambda i: (0, 0))],
        out_specs=(row, row, row),
        compiler_params=_cparams(("parallel",)),
        name="qkv_proj",
    )(h, w_qkv)


def _sb_attn_kernel(q_ref, k_ref, v_ref, o_ref, acc_sc):
    blk = ATT_BLOCK
    npair = q_ref.shape[2] // PAIR
    i = pl.program_id(2)
    first = lax.broadcasted_iota(jnp.int32, (1, PAIR), 1) < HEAD_DIM
    heads = []
    for p in range(npair):
        q = q_ref[0, :, p * PAIR:(p + 1) * PAIR]
        zero = jnp.zeros_like(q)
        heads += [(p, jnp.where(first, q, zero)), (p, jnp.where(first, zero, q))]
    rr = lax.broadcasted_iota(jnp.int32, (blk, blk), 0)
    cc = lax.broadcasted_iota(jnp.int32, (blk, blk), 1)
    causal = cc < rr
    later = (rr > cc).astype(BF16)

    def sweep(j, cs, diagonal):
        start = pl.multiple_of(j * blk, blk)
        kbs = [k_ref[0, pl.ds(start, blk), p * PAIR:(p + 1) * PAIR] for p in range(npair)]
        vbs = [v_ref[0, pl.ds(start, blk), p * PAIR:(p + 1) * PAIR] for p in range(npair)]
        zs = [_dg(qh, kbs[p], _NT) for p, qh in heads]
        sps = [jnp.maximum(z, 0.0) + jnp.log(1.0 + jnp.exp(-jnp.abs(z))) for z in zs]
        us = [jnp.where(causal, sp, 0.0) for sp in sps] if diagonal else sps
        rests = [c - _mm_exact_rhs(u, later) for u, c in zip(us, cs)]
        ws = [jnp.exp((z - sp) + rest) for z, sp, rest in zip(zs, sps, rests)]
        if diagonal:
            ws = [jnp.where(causal, a, 0.0) for a in ws]
        pvs = [jnp.dot(a.astype(BF16), vbs[p], preferred_element_type=F32) for (p, _), a in zip(heads, ws)]
        for p in range(npair):
            pv = jnp.where(first, pvs[2 * p], pvs[2 * p + 1])
            if diagonal:
                acc_sc[:, p * PAIR:(p + 1) * PAIR] = pv
            else:
                acc_sc[:, p * PAIR:(p + 1) * PAIR] += pv
        return tuple(c - jnp.sum(u, axis=1, keepdims=True) for c, u in zip(cs, us))

    def live(cs):
        m = cs[0]
        for c in cs[1:]:
            m = jnp.maximum(m, c)
        return jnp.max(m) > EXP_ZERO_BELOW

    cs = sweep(i, (jnp.zeros((blk, 1), F32),) * len(heads), True)

    def cond(st):
        j, _, alive = st
        return jnp.logical_and(j >= 0, alive)

    def body(st):
        j, cs, _ = st
        cs = sweep(j, cs, False)
        return j - 1, cs, live(cs)

    lax.while_loop(cond, body, (i - 1, cs, live(cs)))
    o_ref[0] = acc_sc[...].astype(o_ref.dtype)


def _sb_attention(q, k, v):
    b, l, w = q.shape
    wb = ATT_PAIRS * PAIR
    return pl.pallas_call(
        _sb_attn_kernel,
        out_shape=jax.ShapeDtypeStruct((b, l, w), BF16),
        grid=(b, w // wb, l // ATT_BLOCK),
        in_specs=[pl.BlockSpec((1, ATT_BLOCK, wb), lambda bi, p, i: (bi, i, p)),
                  pl.BlockSpec((1, l, wb), lambda bi, p, i: (bi, 0, p)),
                  pl.BlockSpec((1, l, wb), lambda bi, p, i: (bi, 0, p))],
        out_specs=pl.BlockSpec((1, ATT_BLOCK, wb), lambda bi, p, i: (bi, i, p)),
        scratch_shapes=[pltpu.VMEM((ATT_BLOCK, wb), F32)],
        compiler_params=_cparams(("parallel", "parallel", "arbitrary")),
        name="sb_attention",
    )(q, k, v)


def _rwkv_prep_kernel(h_ref, w_ref, mu_ref, w0_ref, wup_ref, a0_ref, aup_ref, gup_ref,
                      kk_ref, ka_ref, bd_ref,
                      r_o, lw_o, k_o, v_o, kn_o, b_o, g_o, prev_sc):
    t = pl.program_id(1)
    x = h_ref[0].astype(BF16)
    p = jnp.dot(x, w_ref[...], preferred_element_type=F32)
    rows = p.shape[0]

    @pl.when(t == 0)
    def _():
        prev_sc[...] = jnp.zeros_like(prev_sc)

    row = lax.broadcasted_iota(jnp.int32, (rows, 1), 0)
    shifted = jnp.where(row == 0, prev_sc[...], pltpu.roll(p, 1, axis=0))
    prev_sc[...] = p[rows - 1:rows, :]
    xs = p + (shifted - p) * mu_ref[...]

    w = MIX_WIDTH
    r, k, v = xs[:, :w], xs[:, w:2 * w], xs[:, 2 * w:3 * w]
    wd = xs[:, 3 * w:3 * w + W_LORA]
    ad = xs[:, 3 * w + W_LORA:3 * w + W_LORA + A_LORA]
    gd = xs[:, 3 * w + W_LORA + A_LORA:]

    def lora(x, up_ref):
        return jnp.dot(x.astype(BF16), up_ref[...].astype(BF16), preferred_element_type=F32)

    lw = -DECAY_SCALE * _sigmoid(w0_ref[...] + lora(jnp.tanh(wd), wup_ref))
    a = _sigmoid(a0_ref[...] + lora(ad, aup_ref))
    g = lora(_sigmoid(gd), gup_ref)

    kk = k * kk_ref[...]
    ssq = jnp.dot((kk * kk).astype(BF16), bd_ref[...], preferred_element_type=F32)
    kn = kk / jnp.maximum(jnp.sqrt(ssq), 1e-12)
    r_o[0] = r
    lw_o[0] = lw
    k_o[0] = k * (1.0 + (a - 1.0) * ka_ref[...])
    v_o[0] = v
    kn_o[0] = kn
    b_o[0] = kn * a
    g_o[0] = g


def _rwkv_prep(h3, w_rw, mu, w0, w_up, a0, a_up, g_up, k_k, k_a, bd512):
    b, l, d = h3.shape
    nt = l // ROW_TILE
    out = jax.ShapeDtypeStruct((b, l, MIX_WIDTH), F32)
    row = pl.BlockSpec((1, ROW_TILE, MIX_WIDTH), lambda bi, t: (bi, t, 0))

    def full(arr):
        return pl.BlockSpec(arr.shape, lambda bi, t: (0,) * arr.ndim)

    consts = (w_rw, mu, w0, w_up, a0, a_up, g_up, k_k, k_a, bd512)
    return pl.pallas_call(
        _rwkv_prep_kernel,
        out_shape=(out,) * 7,
        grid=(b, nt),
        in_specs=[pl.BlockSpec((1, ROW_TILE, d), lambda bi, t: (bi, t, 0))] + [full(c) for c in consts],
        out_specs=(row,) * 7,
        scratch_shapes=[pltpu.VMEM((1, RW_WIDTH), F32)],
        compiler_params=_cparams(("parallel", "arbitrary")),
        name="rwkv_prep",
    )(h3, *consts)


def _rwkv_scan_kernel(r_ref, lw_ref, k_ref, v_ref, kn_ref, b_ref, g_ref,
                      rk_ref, lng_ref, lnb_ref, o_ref, state_sc):
    nb, rows, w = r_ref.shape
    c = SCAN_CHUNK
    nchunk = rows // c
    npair = w // PAIR
    assert 2 * c == PAIR

    @pl.when(pl.program_id(0) == 0)
    def _():
        state_sc[...] = jnp.zeros_like(state_sc)

    rr = lax.broadcasted_iota(jnp.int32, (c, c), 0)
    cc = lax.broadcasted_iota(jnp.int32, (c, c), 1)
    ones_incl = (rr >= cc).astype(BF16)
    row_c = lax.broadcasted_iota(jnp.int32, (c, PAIR), 0)
    key_c = lax.broadcasted_iota(jnp.int32, (c, PAIR), 1) & (HEAD_DIM - 1)
    strict_m = key_c < row_c
    incl_m = key_c <= row_c
    first = lax.broadcasted_iota(jnp.int32, (1, PAIR), 1) < HEAD_DIM
    pr = lax.broadcasted_iota(jnp.int32, (PAIR, PAIR), 0)
    pc = lax.broadcasted_iota(jnp.int32, (PAIR, PAIR), 1)
    same_head = (pr < HEAD_DIM) == (pc < HEAD_DIM)
    head0_rows = lax.broadcasted_iota(jnp.int32, (PAIR, 2 * PAIR), 0) < HEAD_DIM
    head_ones = same_head.astype(BF16)
    eye_p = pr == pc
    eye_sbs = (key_c == row_c).astype(F32)
    zeros_c = jnp.zeros((c, PAIR), F32)

    def cat0(*xs):
        return jnp.concatenate(xs, axis=0)

    def cat1(*xs):
        return jnp.concatenate(xs, axis=1)

    def mm(a, b, dims=_NN):
        return _dg(a.astype(BF16), b.astype(BF16), dims)

    streams = [(bi, p, ci) for bi in range(nb) for p in range(npair) for ci in range(nchunk)]

    def sl_of(s):
        bi, p, ci = s
        return (bi, slice(ci * c, (ci + 1) * c), slice(p * PAIR, (p + 1) * PAIR))

    pre = {}
    for bi in range(nb):
        for ci in range(nchunk):
            rs = (bi, slice(ci * c, (ci + 1) * c), slice(None))
            lw = lw_ref[rs]
            cl = _mm_exact_lhs(ones_incl, lw)
            cl_end = cl[c - 1:c, :]
            e_pos = jnp.exp(cl)
            e_neg = jnp.exp(-cl)
            e_end = jnp.exp(cl_end - cl)
            r, k, kn, b = r_ref[rs], k_ref[rs], kn_ref[rs], b_ref[rs]
            full = dict(a_t=-kn * jnp.exp(cl - lw), r_t=r * e_pos, b_t=b * e_neg, k_t=k * e_neg,
                        b_h=b * e_end, k_h=k * e_end, g_end=jnp.exp(cl_end))
            for p in range(npair):
                pre[(bi, p, ci)] = {n: x[:, p * PAIR:(p + 1) * PAIR] for n, x in full.items()}

    tri = {}
    for s in streams:
        d = pre[s]
        a_t, r_t, b_t, k_t = d["a_t"], d["r_t"], d["b_t"], d["k_t"]
        bk = cat0(b_t, k_t)
        s0 = mm(jnp.where(first, cat0(a_t, r_t), 0.0), bk, _NT)
        s1t = mm(jnp.where(first, 0.0, a_t), cat0(k_t, b_t), _NT)
        s1b = mm(jnp.where(first, 0.0, r_t), bk, _NT)
        top0 = jnp.where(strict_m, s0[:c], 0.0)
        top1 = jnp.where(strict_m, s1t, 0.0)
        tri[s] = dict(
            l_ab_sbs=jnp.where(first, top0, top1),
            l_ak=cat0(jnp.where(first, 0.0, top0), jnp.where(first, top1, 0.0)),
            m0=jnp.where(incl_m, s0[c:], 0.0), m1=jnp.where(incl_m, s1b, 0.0))

    xs = {}
    for s in streams:
        v = v_ref[sl_of(s)]
        xs[s] = mm(tri[s]["l_ak"], cat0(v, v))

    def diag2(x):
        return cat0(jnp.where(first, x, 0.0), jnp.where(first, 0.0, x))

    pw = {s: tri[s]["l_ab_sbs"] for s in streams}
    tinv = {s: eye_sbs + pw[s] for s in streams}
    sq = {s: mm(pw[s], diag2(pw[s])) for s in streams}
    n = 2
    while 2 * n < c:
        for s in streams:
            both = mm(cat0(sq[s], tinv[s]), diag2(sq[s]))
            tinv[s] = tinv[s] + both[c:]
            sq[s] = both[:c]
        n *= 2
    for s in streams:
        tinv[s] = diag2(tinv[s] + mm(tinv[s], diag2(sq[s])))

    pq = {}
    for s in streams:
        a_t = pre[s]["a_t"]
        rhs = cat0(cat1(a_t, xs[s][:c]), cat1(a_t, xs[s][c:]))
        pq[s] = mm(tinv[s], rhs)

    coef = {}
    for s in streams:
        d = pre[s]
        v = v_ref[sl_of(s)]
        bkh_t = cat0(d["b_h"], d["k_h"]).T
        zv = cat1(zeros_c, v)
        res0 = mm(cat0(tri[s]["m0"], bkh_t), cat0(pq[s][:c], zv))
        res1 = mm(cat0(tri[s]["m1"], bkh_t), cat0(pq[s][c:], zv))
        ry = d["r_t"] + jnp.where(first, res0[:c, :PAIR], res1[:c, :PAIR])
        y0 = jnp.where(first, res0[:c, PAIR:], res1[:c, PAIR:])
        gj = jnp.where(head0_rows, res0[c:], res1[c:])
        gmat = jnp.where(same_head, gj[:, :PAIR], 0.0) + jnp.where(eye_p, d["g_end"], 0.0)
        jmat = jnp.where(same_head, gj[:, PAIR:], 0.0)
        coef[s] = (cat0(ry, gmat), y0, jmat)

    ys = {}
    for bi in range(nb):
        for p in range(npair):
            idx = bi * npair + p
            state = state_sc[idx]
            for ci in range(nchunk):
                lhs, y0, jmat = coef[(bi, p, ci)]
                st = mm(lhs, state)
                ys[(bi, p, ci)] = st[:c] + y0
                state = st[c:] + jmat
            state_sc[idx] = state

    for s in streams:
        sl = sl_of(s)
        csl = (slice(None), sl[2])
        y = ys[s]
        v = v_ref[sl]
        r, k = r_ref[sl], k_ref[sl]
        mean = mm(y, head_ones) * (1.0 / HEAD_DIM)
        yc = y - mean
        var = mm(yc * yc, head_ones) * (1.0 / HEAD_DIM)
        yn = yc * lax.rsqrt(var + LNX_EPS) * lng_ref[csl] + lnb_ref[csl]
        bonus = mm(r * k * rk_ref[csl], head_ones) * v
        o_ref[sl] = ((yn + bonus) * g_ref[sl]).astype(o_ref.dtype)


def _rwkv_scan(r, lw, k, v, kn, bb, g, r_k, lnx_g, lnx_b):
    b, l, w = r.shape
    c = SCAN_CHUNK * SCAN_CHUNKS_PER_STEP
    row = pl.BlockSpec((b, c, w), lambda i: (0, i, 0))
    vec = pl.BlockSpec((1, w), lambda i: (0, 0))
    return pl.pallas_call(
        _rwkv_scan_kernel,
        out_shape=jax.ShapeDtypeStruct((b, l, w), BF16),
        grid=(l // c,),
        in_specs=[row] * 7 + [vec] * 3,
        out_specs=row,
        scratch_shapes=[pltpu.VMEM((b * (w // PAIR), PAIR, PAIR), F32)],
        compiler_params=_cparams(("arbitrary",)),
        name="rwkv_scan",
    )(r, lw, k, v, kn, bb, g, r_k, lnx_g, lnx_b)


def _merge_kernel(h_ref, osb_ref, orw_ref, wg_ref, psb_ref, prw_ref, wout_ref,
                  g1_ref, b1_ref, rw_ref, rb_ref, h1_ref, comb_ref):
    h = h_ref[...]
    hb = h.astype(BF16)
    gates = _sigmoid(jnp.dot(hb, wg_ref[...], preferred_element_type=F32))
    sb = jnp.dot(osb_ref[...], psb_ref[...], preferred_element_type=F32)
    rw = jnp.dot(orw_ref[...], prw_ref[...], preferred_element_type=F32)
    merged = gates[:, :D_MODEL] * sb + gates[:, D_MODEL:] * rw
    mix = jnp.dot(merged.astype(BF16), wout_ref[...], preferred_element_type=F32)
    h1 = _layer_norm(ALPHA * h + mix, g1_ref[...], b1_ref[...])
    h1_ref[...] = h1

    logits = _mm(h1, rw_ref[...]) + rb_ref[...]
    m = jnp.max(logits, axis=-1, keepdims=True)
    ex = jnp.exp(logits - m)
    probs = ex / jnp.sum(ex, axis=-1, keepdims=True)
    lane = lax.broadcasted_iota(jnp.int32, probs.shape, 1)
    big = N_EXPERTS

    def top2(vals):
        v1 = jnp.max(vals, axis=-1, keepdims=True)
        i1 = jnp.min(jnp.where(vals == v1, lane, big), axis=-1, keepdims=True)
        rest = jnp.where(lane == i1, -2.0, vals)
        v2 = jnp.max(rest, axis=-1, keepdims=True)
        i2 = jnp.min(jnp.where(rest == v2, lane, big), axis=-1, keepdims=True)
        return v1, i1, v2, i2

    best = None
    sel = None
    for gi in range(N_GROUPS):
        in_g = (lane >= gi * GROUP) & (lane < (gi + 1) * GROUP)
        v1, _, v2, _ = top2(jnp.where(in_g, probs, -1.0))
        score = v1 + v2
        if gi == 0:
            best, sel = score, jnp.zeros_like(score, dtype=jnp.int32)
        else:
            better = score > best
            best = jnp.where(better, score, best)
            sel = jnp.where(better, gi, sel)
    in_sel = (lane >= sel * GROUP) & (lane < (sel + 1) * GROUP)
    v1, i1, v2, i2 = top2(jnp.where(in_sel, probs, -1.0))
    denom = v1 + v2
    comb_ref[...] = jnp.where(lane == i1, v1 / denom, 0.0) + jnp.where(lane == i2, v2 / denom, 0.0)


def _merge(h, o_sb, o_rw, w_gates, p_sb, p_rwkv, w_out, ln_g, ln_b, router_w, router_b):
    t, d = h.shape

    def full(arr):
        return pl.BlockSpec(arr.shape, lambda i: (0,) * arr.ndim)

    consts = (w_gates, p_sb, p_rwkv, w_out, ln_g, ln_b, router_w, router_b)
    return pl.pallas_call(
        _merge_kernel,
        out_shape=(jax.ShapeDtypeStruct((t, d), F32), jax.ShapeDtypeStruct((t, N_EXPERTS), F32)),
        grid=(t // ROW_TILE,),
        in_specs=[pl.BlockSpec((ROW_TILE, d), lambda i: (i, 0)),
                  pl.BlockSpec((ROW_TILE, MIX_WIDTH), lambda i: (i, 0)),
                  pl.BlockSpec((ROW_TILE, MIX_WIDTH), lambda i: (i, 0))] + [full(c) for c in consts],
        out_specs=(pl.BlockSpec((ROW_TILE, d), lambda i: (i, 0)),
                   pl.BlockSpec((ROW_TILE, N_EXPERTS), lambda i: (i, 0))),
        compiler_params=_cparams(("parallel",)),
        name="merge_ln1_router",
    )(h, o_sb, o_rw, *consts)


def _moe_kernel(h_ref, comb_ref, wg_ref, wu_ref, wd_ref, g2_ref, b2_ref, o_ref,
                xs_sc, ys_sc, col_sc, seg_sm):
    e = pl.program_id(1)
    tm, d = h_ref.shape
    ne = comb_ref.shape[1]
    blk = MOE_SORT_BLOCK
    chunk = MOE_CHUNK
    ns = xs_sc.shape[0] - chunk

    @pl.when(e == 0)
    def _dispatch():
        comb = comb_ref[...]
        sel = comb != 0.0
        self32 = jnp.where(sel, 1.0, 0.0)
        selb = self32.astype(BF16)
        lane = lax.broadcasted_iota(jnp.int32, (tm, ne), 1)
        ranks = []
        for r0 in range(0, tm, blk):
            rr = lax.broadcasted_iota(jnp.int32, (blk, tm), 0) + r0
            cc = lax.broadcasted_iota(jnp.int32, (blk, tm), 1)
            ranks.append(jnp.dot(jnp.where(cc < rr, 1.0, 0.0).astype(BF16), selb, preferred_element_type=F32))
        rank = jnp.concatenate(ranks, axis=0)
        counts = jnp.sum(self32, axis=0, keepdims=True).astype(jnp.int32)
        lane1 = lax.broadcasted_iota(jnp.int32, (1, ne), 1)
        off = jnp.int32(0)
        offv = jnp.zeros((1, ne), F32)
        for ex in range(ne):
            n = counts[0, ex]
            seg_sm[0, ex] = off
            seg_sm[1, ex] = n
            offv = jnp.where(lane1 == ex, off.astype(F32), offv)
            off = off + ((n + (MOE_ALIGN - 1)) // MOE_ALIGN) * MOE_ALIGN
        pos = offv + rank

        def pick(idx, arr):
            return jnp.sum(jnp.where(lane == idx, arr, 0.0), axis=1, keepdims=True)

        i_lo = jnp.min(jnp.where(sel, lane, ne), axis=1, keepdims=True)
        i_hi = jnp.max(jnp.where(sel, lane, -1), axis=1, keepdims=True)
        any_sel = i_hi >= 0
        two = i_hi != i_lo
        p_lo = jnp.where(any_sel, pick(i_lo, pos), -1.0)
        p_hi = jnp.where(two, pick(i_hi, pos), -1.0)
        g_lo = pick(i_lo, comb)
        g_hi = jnp.where(two, pick(i_hi, comb), 0.0)
        l128 = lax.broadcasted_iota(jnp.int32, (tm, 128), 1)
        cols = jnp.where(l128 == 0, p_lo, jnp.where(l128 == 1, p_hi, jnp.where(l128 == 2, g_lo,
                         jnp.where(l128 == 3, g_hi, 0.0))))
        col_sc[...] = cols
        rows_t = cols.T
        p_lo_r, p_hi_r = rows_t[0:1, :], rows_t[1:2, :]
        x = h_ref[...].astype(BF16)
        for r0 in range(0, ns, blk):
            ri = (lax.broadcasted_iota(jnp.int32, (blk, tm), 0) + r0).astype(F32)
            onehot = jnp.where((ri == p_lo_r) | (ri == p_hi_r), 1.0, 0.0).astype(BF16)
            xs_sc[r0:r0 + blk, :] = jnp.dot(onehot, x, preferred_element_type=F32).astype(BF16)
        xs_sc[ns:, :] = jnp.zeros((chunk, d), BF16)
        ys_sc[...] = jnp.zeros_like(ys_sc)

    seg_off = seg_sm[0, e]
    seg_n = seg_sm[1, e]

    def ffn_chunk(c, carry):
        start = pl.multiple_of(seg_off + c * chunk, MOE_ALIGN)
        x = xs_sc[pl.ds(start, chunk), :]
        gate = jnp.dot(x, wg_ref[0, 0], preferred_element_type=F32)
        up = jnp.dot(x, wu_ref[0, 0], preferred_element_type=F32)
        hid = gate * _sigmoid(gate) * up
        ys_sc[pl.ds(start, chunk), :] = jnp.dot(hid.astype(BF16), wd_ref[0, 0],
                                                preferred_element_type=F32).astype(BF16)
        return carry

    lax.fori_loop(0, (seg_n + (chunk - 1)) // chunk, ffn_chunk, 0)

    @pl.when(e == pl.num_programs(1) - 1)
    def _combine():
        ys = ys_sc[0:ns, :]
        for r0 in range(0, tm, blk):
            cols = col_sc[r0:r0 + blk, :]
            p_lo, p_hi, g_lo, g_hi = cols[:, 0:1], cols[:, 1:2], cols[:, 2:3], cols[:, 3:4]
            li = lax.broadcasted_iota(jnp.int32, (blk, ns), 1).astype(F32)
            wsel = jnp.where(li == p_lo, g_lo, 0.0) + jnp.where(li == p_hi, g_hi, 0.0)
            ff = jnp.dot(wsel.astype(BF16), ys, preferred_element_type=F32)
            o_ref[r0:r0 + blk, :] = _layer_norm(ALPHA * h_ref[r0:r0 + blk, :] + ff, g2_ref[...], b2_ref[...])


def _moe(h, comb, wg, wu, wd, li, ln_g, ln_b):
    t, d = h.shape
    _, ne, _, f = wg.shape
    rows = MOE_ROW_TILE
    ns = -(-(2 * rows + ne * (MOE_ALIGN - 1)) // MOE_SORT_BLOCK) * MOE_SORT_BLOCK
    return pl.pallas_call(
        _moe_kernel,
        out_shape=jax.ShapeDtypeStruct((t, d), F32),
        grid=(t // rows, ne),
        in_specs=[pl.BlockSpec((rows, d), lambda i, e: (i, 0), pipeline_mode=pl.Buffered(1)),
                  pl.BlockSpec((rows, ne), lambda i, e: (i, 0)),
                  pl.BlockSpec((1, 1, d, f), lambda i, e: (li, e, 0, 0)),
                  pl.BlockSpec((1, 1, d, f), lambda i, e: (li, e, 0, 0)),
                  pl.BlockSpec((1, 1, f, d), lambda i, e: (li, e, 0, 0)),
                  pl.BlockSpec((1, d), lambda i, e: (0, 0)),
                  pl.BlockSpec((1, d), lambda i, e: (0, 0))],
        out_specs=pl.BlockSpec((rows, d), lambda i, e: (i, 0)),
        scratch_shapes=[pltpu.VMEM((ns + MOE_CHUNK, d), BF16),
                        pltpu.VMEM((ns + MOE_CHUNK, d), BF16),
                        pltpu.VMEM((rows, 128), F32),
                        pltpu.SMEM((2, ne), jnp.int32)],
        compiler_params=pltpu.CompilerParams(dimension_semantics=("parallel", "arbitrary"),
                                             vmem_limit_bytes=MOE_VMEM_LIMIT),
        name="moe_ln2",
    )(h, comb, wg, wu, wd, ln_g, ln_b)


def _layer(h, lp, shared, batch):
    t, d = h.shape
    l = t // batch
    q, k, v = _qkv_proj(h, lp["w_qkv"])
    shp = (batch, l, MIX_WIDTH)
    o_sb = _sb_attention(q.reshape(shp), k.reshape(shp), v.reshape(shp)).reshape(t, MIX_WIDTH)
    prep = _rwkv_prep(h.reshape(batch, l, d), lp["w_rw"], lp["mu"], lp["w0"], lp["w_up"], lp["a0"],
                      lp["a_up"], lp["g_up"], lp["k_k"], lp["k_a"], shared["bd512"])
    o_rw = _rwkv_scan(*prep, lp["r_k"], lp["lnx_g"], lp["lnx_b"]).reshape(t, MIX_WIDTH)
    h1, comb = _merge(h, o_sb, o_rw, lp["w_gates"], lp["p_sb"], lp["p_rwkv"], lp["w_out"],
                      lp["ln1_g"], lp["ln1_b"], shared["router_w"], shared["router_b"])
    return _moe(h1, comb, shared["wg"], shared["wu"], shared["wd"], lp["li"], lp["ln2_g"], lp["ln2_b"])


def kernel(x, meta, emb_ln_g, emb_ln_b, w_in, rwkv_mu, w0, w_up, a0, a_up, g_up, k_k, k_a, r_k, lnx_g, lnx_b, p_sb, p_rwkv, w_out, ln1_g, ln1_b, router_w, router_b, exp_w_gate, exp_w_up, exp_w_down, ln2_g, ln2_b):
    batch, seq, d = x.shape
    n_meta = meta.shape[0]
    l = seq + n_meta
    lpad = -(-l // SEQ_ALIGN) * SEQ_ALIGN
    depth = w_in.shape[0]

    meta_b = jnp.broadcast_to(meta.astype(x.dtype)[None], (batch, n_meta, d))
    hcat = jnp.concatenate([meta_b, x, jnp.zeros((batch, lpad - l, d), x.dtype)], axis=1)
    h = _embed_ln(hcat.reshape(batch * lpad, d), emb_ln_g.reshape(1, d), emb_ln_b.reshape(1, d))

    idx = jnp.arange(MIX_WIDTH) // HEAD_DIM
    shared = {
        "bd512": (idx[:, None] == idx[None, :]).astype(BF16),
        "router_w": router_w,
        "wg": exp_w_gate.astype(BF16), "wu": exp_w_up.astype(BF16), "wd": exp_w_down.astype(BF16),
        "router_b": router_b.reshape(1, -1),
    }
    qkv_w = 3 * MIX_WIDTH
    for li in range(depth):
        row = lambda a: a[li].reshape(1, -1)
        lp = {
            "w_qkv": w_in[li, :, :qkv_w].astype(BF16),
            "w_rw": w_in[li, :, qkv_w:qkv_w + RW_WIDTH].astype(BF16),
            "w_gates": w_in[li, :, qkv_w + RW_WIDTH:].astype(BF16),
            "mu": row(rwkv_mu), "w0": row(w0), "a0": row(a0), "k_k": row(k_k), "k_a": row(k_a),
            "w_up": w_up[li], "a_up": a_up[li], "g_up": g_up[li],
            "r_k": row(r_k), "lnx_g": row(lnx_g), "lnx_b": row(lnx_b),
            "p_sb": p_sb[li].astype(BF16), "p_rwkv": p_rwkv[li].astype(BF16),
            "w_out": w_out[li].astype(BF16),
            "ln1_g": row(ln1_g), "ln1_b": row(ln1_b), "ln2_g": row(ln2_g), "ln2_b": row(ln2_b),
            "li": li,
        }
        h = _layer(h, lp, shared, batch)
    return h.reshape(batch, lpad, d)[:, n_meta:l]
```

```python
import functools
import math

import jax
import jax.numpy as jnp
from jax import lax
from jax.experimental import pallas as pl
from jax.experimental.pallas import tpu as pltpu

F32 = jnp.float32
BF16 = jnp.bfloat16

D_MODEL = 1024
DEPTH = 2
N_META = 16
HEAD_DIM = 64
PAIR = 2 * HEAD_DIM
MIX_WIDTH = 512
W_LORA, A_LORA, G_LORA = 64, 64, 128
RW_WIDTH = 3 * MIX_WIDTH + W_LORA + A_LORA + G_LORA
DECAY_SCALE = math.exp(-0.5)
LNX_EPS = 64e-5
LN_EPS = 1e-5
N_EXPERTS = 16
N_GROUPS = 4
GROUP = N_EXPERTS // N_GROUPS
D_EXPERT = 512
ALPHA = (2 * DEPTH) ** 0.25

SEQ_ALIGN = 640
ROW_TILE = 640
ATT_BLOCK = 128
ATT_SUB = 32
ATT_PAIRS = 4
SCAN_CHUNK = 64
SCAN_CHUNKS_PER_STEP = 2
EXP_ZERO_BELOW = -104.0
MOE_ROW_TILE = 1280
MOE_CHUNK = 192
MOE_SORT_BLOCK = 256
MOE_EXPERTS_PER_STEP = 2
MOE_ALIGN = 16
VMEM_LIMIT = 48 * 1024 * 1024
MOE_VMEM_LIMIT = 58 * 1024 * 1024


def _cparams(sem):
    return pltpu.CompilerParams(dimension_semantics=sem, vmem_limit_bytes=VMEM_LIMIT)


def _split2(x):
    hi = x.astype(BF16)
    lo = (x - hi.astype(F32)).astype(BF16)
    return hi, lo


def _split3(x):
    hi = x.astype(BF16)
    r1 = x - hi.astype(F32)
    mid = r1.astype(BF16)
    lo = (r1 - mid.astype(F32)).astype(BF16)
    return hi, mid, lo


_NN = (((1,), (0,)), ((), ()))
_NT = (((1,), (1,)), ((), ()))


def _dg(a, b, dims):
    return lax.dot_general(a, b, dims, preferred_element_type=F32)


def _mm(a, b, dims=_NN):
    ah, al = _split2(a)
    bh, bl = _split2(b)
    return _dg(ah, bh, dims) + (_dg(ah, bl, dims) + _dg(al, bh, dims))


def _mm_exact_lhs(a_bf16, b, dims=_NN):
    b1, b2, b3 = _split3(b)
    return _dg(a_bf16, b1, dims) + (_dg(a_bf16, b2, dims) + _dg(a_bf16, b3, dims))


def _mm_exact_rhs(a, b_bf16, dims=_NN):
    a1, a2 = _split2(a)
    return _dg(a1, b_bf16, dims) + _dg(a2, b_bf16, dims)


def _sigmoid(x):
    return 1.0 / (1.0 + jnp.exp(-x))


def _layer_norm(x, g, b):
    mu = jnp.mean(x, axis=-1, keepdims=True)
    xc = x - mu
    var = jnp.mean(xc * xc, axis=-1, keepdims=True)
    return xc * lax.rsqrt(var + LN_EPS) * g + b


def _embed_ln_kernel(x_ref, g_ref, b_ref, o_ref):
    o_ref[...] = _layer_norm(x_ref[...], g_ref[...], b_ref[...])


def _embed_ln(x, g, b):
    t, d = x.shape
    return pl.pallas_call(
        _embed_ln_kernel,
        out_shape=jax.ShapeDtypeStruct((t, d), F32),
        grid=(t // ROW_TILE,),
        in_specs=[pl.BlockSpec((ROW_TILE, d), lambda i: (i, 0)),
                  pl.BlockSpec((1, d), lambda i: (0, 0)),
                  pl.BlockSpec((1, d), lambda i: (0, 0))],
        out_specs=pl.BlockSpec((ROW_TILE, d), lambda i: (i, 0)),
        compiler_params=_cparams(("parallel",)),
        name="embed_ln",
    )(x, g, b)


def _qkv_kernel(h_ref, w_ref, q_ref, k_ref, v_ref):
    x = h_ref[...].astype(BF16)
    p = jnp.dot(x, w_ref[...], preferred_element_type=F32)
    q_ref[...] = (p[:, :MIX_WIDTH] * (HEAD_DIM ** -0.5)).astype(BF16)
    k_ref[...] = p[:, MIX_WIDTH:2 * MIX_WIDTH].astype(BF16)
    v_ref[...] = p[:, 2 * MIX_WIDTH:].astype(BF16)


def _qkv_proj(h, w_qkv):
    t, d = h.shape
    out = jax.ShapeDtypeStruct((t, MIX_WIDTH), BF16)
    row = pl.BlockSpec((ROW_TILE, MIX_WIDTH), lambda i: (i, 0))
    return pl.pallas_call(
        _qkv_kernel,
        out_shape=(out, out, out),
        grid=(t // ROW_TILE,),
        in_specs=[pl.BlockSpec((ROW_TILE, d), lambda i: (i, 0)),
                  pl.BlockSpec((d, 3 * MIX_WIDTH), lambda i: (0, 0))],
        out_specs=(row, row, row),
        compiler_params=_cparams(("parallel",)),
        name="qkv_proj",
    )(h, w_qkv)


def _sb_attn_kernel(q_ref, k_ref, v_ref, o_ref, acc_sc, c_sc):
    blk = ATT_BLOCK
    sub = ATT_SUB
    npair = q_ref.shape[2] // PAIR
    i = pl.program_id(2)
    first = lax.broadcasted_iota(jnp.int32, (1, PAIR), 1) < HEAD_DIM
    heads = []
    for p in range(npair):
        q = q_ref[0, :, p * PAIR:(p + 1) * PAIR]
        zero = jnp.zeros_like(q)
        heads += [(p, jnp.where(first, q, zero)), (p, jnp.where(first, zero, q))]
    rr = lax.broadcasted_iota(jnp.int32, (blk, blk), 0)
    cc = lax.broadcasted_iota(jnp.int32, (blk, blk), 1)
    causal = cc < rr
    later = (rr > cc).astype(BF16)

    def sweep(j, r0, nrows, diagonal):
        rows = slice(r0, r0 + nrows)
        start = pl.multiple_of(j * blk, blk)
        kbs = [k_ref[0, pl.ds(start, blk), p * PAIR:(p + 1) * PAIR] for p in range(npair)]
        vbs = [v_ref[0, pl.ds(start, blk), p * PAIR:(p + 1) * PAIR] for p in range(npair)]
        zs = [_dg(qh[rows], kbs[p], _NT) for p, qh in heads]
        sps = [jnp.maximum(z, 0.0) + jnp.log(1.0 + jnp.exp(-jnp.abs(z))) for z in zs]
        if diagonal:
            us = [jnp.where(causal[rows], sp, 0.0) for sp in sps]
            cs = [jnp.zeros((nrows, 1), F32)] * len(heads)
        else:
            us = sps
            cs = [c_sc[h, rows, :] for h in range(len(heads))]
        rests = [c - _mm_exact_rhs(u, later) for u, c in zip(us, cs)]
        ws = [jnp.exp((z - sp) + rest) for z, sp, rest in zip(zs, sps, rests)]
        if diagonal:
            ws = [jnp.where(causal[rows], a, 0.0) for a in ws]
        pvs = [jnp.dot(a.astype(BF16), vbs[p], preferred_element_type=F32) for (p, _), a in zip(heads, ws)]
        for p in range(npair):
            pv = jnp.where(first, pvs[2 * p], pvs[2 * p + 1])
            if diagonal:
                acc_sc[rows, p * PAIR:(p + 1) * PAIR] = pv
            else:
                acc_sc[rows, p * PAIR:(p + 1) * PAIR] += pv
        cs = [c - jnp.sum(u, axis=1, keepdims=True) for c, u in zip(cs, us)]
        for h, c in enumerate(cs):
            c_sc[h, rows, :] = c
        return cs

    def live(cs, r0, nrows):
        m = cs[0][r0:r0 + nrows]
        for c in cs[1:]:
            m = jnp.maximum(m, c[r0:r0 + nrows])
        return (jnp.max(m) > EXP_ZERO_BELOW).astype(jnp.int32)

    nsub = blk // sub
    cs = sweep(i, 0, blk, True)
    alive0 = tuple(live(cs, g * sub, sub) for g in range(nsub))

    def cond(st):
        j, alive = st
        any_alive = alive[0]
        for a in alive[1:]:
            any_alive = jnp.maximum(any_alive, a)
        return jnp.logical_and(j >= 0, any_alive > 0)

    def body(st):
        j, alive = st
        all_alive = alive[0]
        for a in alive[1:]:
            all_alive = jnp.minimum(all_alive, a)

        def whole():
            cs = sweep(j, 0, blk, False)
            return tuple(live(cs, g * sub, sub) for g in range(nsub))

        def by_group():
            return tuple(lax.cond(alive[g] > 0,
                                  lambda g=g: live(sweep(j, g * sub, sub, False), 0, sub),
                                  lambda: jnp.int32(0)) for g in range(nsub))

        return j - 1, lax.cond(all_alive > 0, whole, by_group)

    lax.while_loop(cond, body, (i - 1, alive0))
    o_ref[0] = acc_sc[...].astype(o_ref.dtype)


def _sb_attention(q, k, v):
    b, l, w = q.shape
    wb = ATT_PAIRS * PAIR
    return pl.pallas_call(
        _sb_attn_kernel,
        out_shape=jax.ShapeDtypeStruct((b, l, w), BF16),
        grid=(b, w // wb, l // ATT_BLOCK),
        in_specs=[pl.BlockSpec((1, ATT_BLOCK, wb), lambda bi, p, i: (bi, i, p)),
                  pl.BlockSpec((1, l, wb), lambda bi, p, i: (bi, 0, p)),
                  pl.BlockSpec((1, l, wb), lambda bi, p, i: (bi, 0, p))],
        out_specs=pl.BlockSpec((1, ATT_BLOCK, wb), lambda bi, p, i: (bi, i, p)),
        scratch_shapes=[pltpu.VMEM((ATT_BLOCK, wb), F32),
                        pltpu.VMEM((wb // HEAD_DIM, ATT_BLOCK, 1), F32)],
        compiler_params=_cparams(("parallel", "parallel", "arbitrary")),
        name="sb_attention",
    )(q, k, v)


def _rwkv_prep_kernel(h_ref, w_ref, mu_ref, w0_ref, wup_ref, a0_ref, aup_ref, gup_ref,
                      kk_ref, ka_ref, bd_ref,
                      r_o, lw_o, k_o, v_o, kn_o, b_o, g_o, prev_sc):
    t = pl.program_id(1)
    x = h_ref[0].astype(BF16)
    p = jnp.dot(x, w_ref[...], preferred_element_type=F32)
    rows = p.shape[0]

    @pl.when(t == 0)
    def _():
        prev_sc[...] = jnp.zeros_like(prev_sc)

    row = lax.broadcasted_iota(jnp.int32, (rows, 1), 0)
    shifted = jnp.where(row == 0, prev_sc[...], pltpu.roll(p, 1, axis=0))
    prev_sc[...] = p[rows - 1:rows, :]
    xs = p + (shifted - p) * mu_ref[...]

    w = MIX_WIDTH
    r, k, v = xs[:, :w], xs[:, w:2 * w], xs[:, 2 * w:3 * w]
    wd = xs[:, 3 * w:3 * w + W_LORA]
    ad = xs[:, 3 * w + W_LORA:3 * w + W_LORA + A_LORA]
    gd = xs[:, 3 * w + W_LORA + A_LORA:]

    def lora(x, up_ref):
        return jnp.dot(x.astype(BF16), up_ref[...].astype(BF16), preferred_element_type=F32)

    lw = -DECAY_SCALE * _sigmoid(w0_ref[...] + lora(jnp.tanh(wd), wup_ref))
    a = _sigmoid(a0_ref[...] + lora(ad, aup_ref))
    g = lora(_sigmoid(gd), gup_ref)

    kk = k * kk_ref[...]
    ssq = jnp.dot((kk * kk).astype(BF16), bd_ref[...], preferred_element_type=F32)
    kn = kk / jnp.maximum(jnp.sqrt(ssq), 1e-12)
    r_o[0] = r
    lw_o[0] = lw
    k_o[0] = k * (1.0 + (a - 1.0) * ka_ref[...])
    v_o[0] = v
    kn_o[0] = kn
    b_o[0] = kn * a
    g_o[0] = g


def _rwkv_prep(h3, w_rw, mu, w0, w_up, a0, a_up, g_up, k_k, k_a, bd512):
    b, l, d = h3.shape
    nt = l // ROW_TILE
    out = jax.ShapeDtypeStruct((b, l, MIX_WIDTH), F32)
    row = pl.BlockSpec((1, ROW_TILE, MIX_WIDTH), lambda bi, t: (bi, t, 0))

    def full(arr):
        return pl.BlockSpec(arr.shape, lambda bi, t: (0,) * arr.ndim)

    consts = (w_rw, mu, w0, w_up, a0, a_up, g_up, k_k, k_a, bd512)
    return pl.pallas_call(
        _rwkv_prep_kernel,
        out_shape=(out,) * 7,
        grid=(b, nt),
        in_specs=[pl.BlockSpec((1, ROW_TILE, d), lambda bi, t: (bi, t, 0))] + [full(c) for c in consts],
        out_specs=(row,) * 7,
        scratch_shapes=[pltpu.VMEM((1, RW_WIDTH), F32)],
        compiler_params=_cparams(("parallel", "arbitrary")),
        name="rwkv_prep",
    )(h3, *consts)


def _rwkv_scan_kernel(r_ref, lw_ref, k_ref, v_ref, kn_ref, b_ref, g_ref,
                      rk_ref, lng_ref, lnb_ref, o_ref, state_sc):
    nb, rows, w = r_ref.shape
    c = SCAN_CHUNK
    nchunk = rows // c
    npair = w // PAIR
    assert 2 * c == PAIR

    @pl.when(pl.program_id(0) == 0)
    def _():
        state_sc[...] = jnp.zeros_like(state_sc)

    rr = lax.broadcasted_iota(jnp.int32, (c, c), 0)
    cc = lax.broadcasted_iota(jnp.int32, (c, c), 1)
    ones_incl = (rr >= cc).astype(BF16)
    row_c = lax.broadcasted_iota(jnp.int32, (c, PAIR), 0)
    key_c = lax.broadcasted_iota(jnp.int32, (c, PAIR), 1) & (HEAD_DIM - 1)
    strict_m = key_c < row_c
    incl_m = key_c <= row_c
    first = lax.broadcasted_iota(jnp.int32, (1, PAIR), 1) < HEAD_DIM
    pr = lax.broadcasted_iota(jnp.int32, (PAIR, PAIR), 0)
    pc = lax.broadcasted_iota(jnp.int32, (PAIR, PAIR), 1)
    same_head = (pr < HEAD_DIM) == (pc < HEAD_DIM)
    head0_rows = lax.broadcasted_iota(jnp.int32, (PAIR, 2 * PAIR), 0) < HEAD_DIM
    head_ones = same_head.astype(BF16)
    eye_p = pr == pc
    eye_sbs = (key_c == row_c).astype(F32)
    zeros_c = jnp.zeros((c, PAIR), F32)

    def cat0(*xs):
        return jnp.concatenate(xs, axis=0)

    def cat1(*xs):
        return jnp.concatenate(xs, axis=1)

    def mm(a, b, dims=_NN):
        return _dg(a.astype(BF16), b.astype(BF16), dims)

    streams = [(bi, p, ci) for bi in range(nb) for p in range(npair) for ci in range(nchunk)]

    def sl_of(s):
        bi, p, ci = s
        return (bi, slice(ci * c, (ci + 1) * c), slice(p * PAIR, (p + 1) * PAIR))

    pre = {}
    for bi in range(nb):
        for ci in range(nchunk):
            rs = (bi, slice(ci * c, (ci + 1) * c), slice(None))
            lw = lw_ref[rs]
            cl = _mm_exact_lhs(ones_incl, lw)
            cl_end = cl[c - 1:c, :]
            e_pos = jnp.exp(cl)
            e_neg = jnp.exp(-cl)
            e_end = jnp.exp(cl_end - cl)
            r, k, kn, b = r_ref[rs], k_ref[rs], kn_ref[rs], b_ref[rs]
            full = dict(a_t=-kn * jnp.exp(cl - lw), r_t=r * e_pos, b_t=b * e_neg, k_t=k * e_neg,
                        b_h=b * e_end, k_h=k * e_end, g_end=jnp.exp(cl_end))
            for p in range(npair):
                pre[(bi, p, ci)] = {n: x[:, p * PAIR:(p + 1) * PAIR] for n, x in full.items()}

    tri = {}
    for s in streams:
        d = pre[s]
        a_t, r_t, b_t, k_t = d["a_t"], d["r_t"], d["b_t"], d["k_t"]
        bk = cat0(b_t, k_t)
        s0 = mm(jnp.where(first, cat0(a_t, r_t), 0.0), bk, _NT)
        s1t = mm(jnp.where(first, 0.0, a_t), cat0(k_t, b_t), _NT)
        s1b = mm(jnp.where(first, 0.0, r_t), bk, _NT)
        top0 = jnp.where(strict_m, s0[:c], 0.0)
        top1 = jnp.where(strict_m, s1t, 0.0)
        tri[s] = dict(
            l_ab_sbs=jnp.where(first, top0, top1),
            l_ak=cat0(jnp.where(first, 0.0, top0), jnp.where(first, top1, 0.0)),
            m0=jnp.where(incl_m, s0[c:], 0.0), m1=jnp.where(incl_m, s1b, 0.0))

    xs = {}
    for s in streams:
        v = v_ref[sl_of(s)]
        xs[s] = mm(tri[s]["l_ak"], cat0(v, v))

    def diag2(x):
        return cat0(jnp.where(first, x, 0.0), jnp.where(first, 0.0, x))

    pw = {s: tri[s]["l_ab_sbs"] for s in streams}
    tinv = {s: eye_sbs + pw[s] for s in streams}
    sq = {s: mm(pw[s], diag2(pw[s])) for s in streams}
    n = 2
    while 2 * n < c:
        for s in streams:
            both = mm(cat0(sq[s], tinv[s]), diag2(sq[s]))
            tinv[s] = tinv[s] + both[c:]
            sq[s] = both[:c]
        n *= 2
    for s in streams:
        tinv[s] = diag2(tinv[s] + mm(tinv[s], diag2(sq[s])))

    pq = {}
    for s in streams:
        a_t = pre[s]["a_t"]
        rhs = cat0(cat1(a_t, xs[s][:c]), cat1(a_t, xs[s][c:]))
        pq[s] = mm(tinv[s], rhs)

    coef = {}
    for s in streams:
        d = pre[s]
        v = v_ref[sl_of(s)]
        bkh_t = cat0(d["b_h"], d["k_h"]).T
        zv = cat1(zeros_c, v)
        res0 = mm(cat0(tri[s]["m0"], bkh_t), cat0(pq[s][:c], zv))
        res1 = mm(cat0(tri[s]["m1"], bkh_t), cat0(pq[s][c:], zv))
        ry = d["r_t"] + jnp.where(first, res0[:c, :PAIR], res1[:c, :PAIR])
        y0 = jnp.where(first, res0[:c, PAIR:], res1[:c, PAIR:])
        gj = jnp.where(head0_rows, res0[c:], res1[c:])
        gmat = jnp.where(same_head, gj[:, :PAIR], 0.0) + jnp.where(eye_p, d["g_end"], 0.0)
        jmat = jnp.where(same_head, gj[:, PAIR:], 0.0)
        coef[s] = (cat0(ry, gmat), y0, jmat)

    ys = {}
    for bi in range(nb):
        for p in range(npair):
            idx = bi * npair + p
            state = state_sc[idx]
            for ci in range(nchunk):
                lhs, y0, jmat = coef[(bi, p, ci)]
                st = mm(lhs, state)
                ys[(bi, p, ci)] = st[:c] + y0
                state = st[c:] + jmat
            state_sc[idx] = state

    for s in streams:
        sl = sl_of(s)
        csl = (slice(None), sl[2])
        y = ys[s]
        v = v_ref[sl]
        r, k = r_ref[sl], k_ref[sl]
        mean = mm(y, head_ones) * (1.0 / HEAD_DIM)
        yc = y - mean
        var = mm(yc * yc, head_ones) * (1.0 / HEAD_DIM)
        yn = yc * lax.rsqrt(var + LNX_EPS) * lng_ref[csl] + lnb_ref[csl]
        bonus = mm(r * k * rk_ref[csl], head_ones) * v
        o_ref[sl] = ((yn + bonus) * g_ref[sl]).astype(o_ref.dtype)


def _rwkv_scan(r, lw, k, v, kn, bb, g, r_k, lnx_g, lnx_b):
    b, l, w = r.shape
    c = SCAN_CHUNK * SCAN_CHUNKS_PER_STEP
    row = pl.BlockSpec((b, c, w), lambda i: (0, i, 0))
    vec = pl.BlockSpec((1, w), lambda i: (0, 0))
    return pl.pallas_call(
        _rwkv_scan_kernel,
        out_shape=jax.ShapeDtypeStruct((b, l, w), BF16),
        grid=(l // c,),
        in_specs=[row] * 7 + [vec] * 3,
        out_specs=row,
        scratch_shapes=[pltpu.VMEM((b * (w // PAIR), PAIR, PAIR), F32)],
        compiler_params=_cparams(("arbitrary",)),
        name="rwkv_scan",
    )(r, lw, k, v, kn, bb, g, r_k, lnx_g, lnx_b)


def _merge_kernel(h_ref, osb_ref, orw_ref, wg_ref, psb_ref, prw_ref, wout_ref,
                  g1_ref, b1_ref, rw_ref, rb_ref, h1_ref, comb_ref):
    h = h_ref[...]
    hb = h.astype(BF16)
    gates = _sigmoid(jnp.dot(hb, wg_ref[...], preferred_element_type=F32))
    sb = jnp.dot(osb_ref[...], psb_ref[...], preferred_element_type=F32)
    rw = jnp.dot(orw_ref[...], prw_ref[...], preferred_element_type=F32)
    merged = gates[:, :D_MODEL] * sb + gates[:, D_MODEL:] * rw
    mix = jnp.dot(merged.astype(BF16), wout_ref[...], preferred_element_type=F32)
    h1 = _layer_norm(ALPHA * h + mix, g1_ref[...], b1_ref[...])
    h1_ref[...] = h1

    logits = _mm(h1, rw_ref[...]) + rb_ref[...]
    m = jnp.max(logits, axis=-1, keepdims=True)
    ex = jnp.exp(logits - m)
    probs = ex / jnp.sum(ex, axis=-1, keepdims=True)
    lane = lax.broadcasted_iota(jnp.int32, probs.shape, 1)
    big = N_EXPERTS

    def top2(vals):
        v1 = jnp.max(vals, axis=-1, keepdims=True)
        i1 = jnp.min(jnp.where(vals == v1, lane, big), axis=-1, keepdims=True)
        rest = jnp.where(lane == i1, -2.0, vals)
        v2 = jnp.max(rest, axis=-1, keepdims=True)
        i2 = jnp.min(jnp.where(rest == v2, lane, big), axis=-1, keepdims=True)
        return v1, i1, v2, i2

    best = None
    sel = None
    for gi in range(N_GROUPS):
        in_g = (lane >= gi * GROUP) & (lane < (gi + 1) * GROUP)
        v1, _, v2, _ = top2(jnp.where(in_g, probs, -1.0))
        score = v1 + v2
        if gi == 0:
            best, sel = score, jnp.zeros_like(score, dtype=jnp.int32)
        else:
            better = score > best
            best = jnp.where(better, score, best)
            sel = jnp.where(better, gi, sel)
    in_sel = (lane >= sel * GROUP) & (lane < (sel + 1) * GROUP)
    v1, i1, v2, i2 = top2(jnp.where(in_sel, probs, -1.0))
    denom = v1 + v2
    comb_ref[...] = jnp.where(lane == i1, v1 / denom, 0.0) + jnp.where(lane == i2, v2 / denom, 0.0)


def _merge(h, o_sb, o_rw, w_gates, p_sb, p_rwkv, w_out, ln_g, ln_b, router_w, router_b):
    t, d = h.shape

    def full(arr):
        return pl.BlockSpec(arr.shape, lambda i: (0,) * arr.ndim)

    consts = (w_gates, p_sb, p_rwkv, w_out, ln_g, ln_b, router_w, router_b)
    return pl.pallas_call(
        _merge_kernel,
        out_shape=(jax.ShapeDtypeStruct((t, d), F32), jax.ShapeDtypeStruct((t, N_EXPERTS), F32)),
        grid=(t // ROW_TILE,),
        in_specs=[pl.BlockSpec((ROW_TILE, d), lambda i: (i, 0)),
                  pl.BlockSpec((ROW_TILE, MIX_WIDTH), lambda i: (i, 0)),
                  pl.BlockSpec((ROW_TILE, MIX_WIDTH), lambda i: (i, 0))] + [full(c) for c in consts],
        out_specs=(pl.BlockSpec((ROW_TILE, d), lambda i: (i, 0)),
                   pl.BlockSpec((ROW_TILE, N_EXPERTS), lambda i: (i, 0))),
        compiler_params=_cparams(("parallel",)),
        name="merge_ln1_router",
    )(h, o_sb, o_rw, *consts)


def _moe_kernel(h_ref, comb_ref, wg_ref, wu_ref, wd_ref, g2_ref, b2_ref, o_ref,
                xs_sc, ys_sc, col_sc, seg_sm):
    e = pl.program_id(1)
    tm, d = h_ref.shape
    ne = comb_ref.shape[1]
    blk = MOE_SORT_BLOCK
    chunk = MOE_CHUNK
    ns = xs_sc.shape[0] - chunk

    @pl.when(e == 0)
    def _dispatch():
        comb = comb_ref[...]
        sel = comb != 0.0
        self32 = jnp.where(sel, 1.0, 0.0)
        selb = self32.astype(BF16)
        lane = lax.broadcasted_iota(jnp.int32, (tm, ne), 1)
        ranks = []
        for r0 in range(0, tm, blk):
            rr = lax.broadcasted_iota(jnp.int32, (blk, tm), 0) + r0
            cc = lax.broadcasted_iota(jnp.int32, (blk, tm), 1)
            ranks.append(jnp.dot(jnp.where(cc < rr, 1.0, 0.0).astype(BF16), selb, preferred_element_type=F32))
        rank = jnp.concatenate(ranks, axis=0)
        counts = jnp.sum(self32, axis=0, keepdims=True).astype(jnp.int32)
        lane1 = lax.broadcasted_iota(jnp.int32, (1, ne), 1)
        off = jnp.int32(0)
        offv = jnp.zeros((1, ne), F32)
        for ex in range(ne):
            n = counts[0, ex]
            seg_sm[0, ex] = off
            seg_sm[1, ex] = n
            offv = jnp.where(lane1 == ex, off.astype(F32), offv)
            off = off + ((n + (MOE_ALIGN - 1)) // MOE_ALIGN) * MOE_ALIGN
        pos = offv + rank

        def pick(idx, arr):
            return jnp.sum(jnp.where(lane == idx, arr, 0.0), axis=1, keepdims=True)

        i_lo = jnp.min(jnp.where(sel, lane, ne), axis=1, keepdims=True)
        i_hi = jnp.max(jnp.where(sel, lane, -1), axis=1, keepdims=True)
        any_sel = i_hi >= 0
        two = i_hi != i_lo
        p_lo = jnp.where(any_sel, pick(i_lo, pos), -1.0)
        p_hi = jnp.where(two, pick(i_hi, pos), -1.0)
        g_lo = pick(i_lo, comb)
        g_hi = jnp.where(two, pick(i_hi, comb), 0.0)
        l128 = lax.broadcasted_iota(jnp.int32, (tm, 128), 1)
        cols = jnp.where(l128 == 0, p_lo, jnp.where(l128 == 1, p_hi, jnp.where(l128 == 2, g_lo,
                         jnp.where(l128 == 3, g_hi, 0.0))))
        col_sc[...] = cols
        rows_t = cols.T
        p_lo_r, p_hi_r = rows_t[0:1, :], rows_t[1:2, :]
        x = h_ref[...].astype(BF16)
        for r0 in range(0, ns, blk):
            ri = (lax.broadcasted_iota(jnp.int32, (blk, tm), 0) + r0).astype(F32)
            onehot = jnp.where((ri == p_lo_r) | (ri == p_hi_r), 1.0, 0.0).astype(BF16)
            xs_sc[r0:r0 + blk, :] = jnp.dot(onehot, x, preferred_element_type=F32).astype(BF16)
        xs_sc[ns:, :] = jnp.zeros((chunk, d), BF16)
        ys_sc[...] = jnp.zeros_like(ys_sc)

    for sub in range(wg_ref.shape[1]):
        ex = e * wg_ref.shape[1] + sub
        seg_off = seg_sm[0, ex]
        seg_n = seg_sm[1, ex]

        def ffn_chunk(c, carry, sub=sub, seg_off=seg_off):
            start = pl.multiple_of(seg_off + c * chunk, MOE_ALIGN)
            x = xs_sc[pl.ds(start, chunk), :]
            gate = jnp.dot(x, wg_ref[0, sub], preferred_element_type=F32)
            up = jnp.dot(x, wu_ref[0, sub], preferred_element_type=F32)
            hid = gate * _sigmoid(gate) * up
            ys_sc[pl.ds(start, chunk), :] = jnp.dot(hid.astype(BF16), wd_ref[0, sub],
                                                    preferred_element_type=F32).astype(BF16)
            return carry

        lax.fori_loop(0, (seg_n + (chunk - 1)) // chunk, ffn_chunk, 0)

    @pl.when(e == pl.num_programs(1) - 1)
    def _combine():
        ys = ys_sc[0:ns, :]
        for r0 in range(0, tm, blk):
            cols = col_sc[r0:r0 + blk, :]
            p_lo, p_hi, g_lo, g_hi = cols[:, 0:1], cols[:, 1:2], cols[:, 2:3], cols[:, 3:4]
            li = lax.broadcasted_iota(jnp.int32, (blk, ns), 1).astype(F32)
            wsel = jnp.where(li == p_lo, g_lo, 0.0) + jnp.where(li == p_hi, g_hi, 0.0)
            ff = jnp.dot(wsel.astype(BF16), ys, preferred_element_type=F32)
            o_ref[r0:r0 + blk, :] = _layer_norm(ALPHA * h_ref[r0:r0 + blk, :] + ff, g2_ref[...], b2_ref[...])


def _moe(h, comb, wg, wu, wd, li, ln_g, ln_b):
    t, d = h.shape
    _, ne, _, f = wg.shape
    rows = MOE_ROW_TILE
    eps = MOE_EXPERTS_PER_STEP
    ns = -(-(2 * rows + ne * (MOE_ALIGN - 1)) // MOE_SORT_BLOCK) * MOE_SORT_BLOCK
    return pl.pallas_call(
        _moe_kernel,
        out_shape=jax.ShapeDtypeStruct((t, d), F32),
        grid=(t // rows, ne // eps),
        in_specs=[pl.BlockSpec((rows, d), lambda i, e: (i, 0), pipeline_mode=pl.Buffered(1)),
                  pl.BlockSpec((rows, ne), lambda i, e: (i, 0)),
                  pl.BlockSpec((1, eps, d, f), lambda i, e: (li, e, 0, 0)),
                  pl.BlockSpec((1, eps, d, f), lambda i, e: (li, e, 0, 0)),
                  pl.BlockSpec((1, eps, f, d), lambda i, e: (li, e, 0, 0)),
                  pl.BlockSpec((1, d), lambda i, e: (0, 0)),
                  pl.BlockSpec((1, d), lambda i, e: (0, 0))],
        out_specs=pl.BlockSpec((rows, d), lambda i, e: (i, 0)),
        scratch_shapes=[pltpu.VMEM((ns + MOE_CHUNK, d), BF16),
                        pltpu.VMEM((ns + MOE_CHUNK, d), BF16),
                        pltpu.VMEM((rows, 128), F32),
                        pltpu.SMEM((2, ne), jnp.int32)],
        compiler_params=pltpu.CompilerParams(dimension_semantics=("parallel", "arbitrary"),
                                             vmem_limit_bytes=MOE_VMEM_LIMIT),
        name="moe_ln2",
    )(h, comb, wg, wu, wd, ln_g, ln_b)


def _layer(h, lp, shared, batch):
    t, d = h.shape
    l = t // batch
    q, k, v = _qkv_proj(h, lp["w_qkv"])
    shp = (batch, l, MIX_WIDTH)
    o_sb = _sb_attention(q.reshape(shp), k.reshape(shp), v.reshape(shp)).reshape(t, MIX_WIDTH)
    prep = _rwkv_prep(h.reshape(batch, l, d), lp["w_rw"], lp["mu"], lp["w0"], lp["w_up"], lp["a0"],
                      lp["a_up"], lp["g_up"], lp["k_k"], lp["k_a"], shared["bd512"])
    o_rw = _rwkv_scan(*prep, lp["r_k"], lp["lnx_g"], lp["lnx_b"]).reshape(t, MIX_WIDTH)
    h1, comb = _merge(h, o_sb, o_rw, lp["w_gates"], lp["p_sb"], lp["p_rwkv"], lp["w_out"],
                      lp["ln1_g"], lp["ln1_b"], shared["router_w"], shared["router_b"])
    return _moe(h1, comb, shared["wg"], shared["wu"], shared["wd"], lp["li"], lp["ln2_g"], lp["ln2_b"])


def kernel(x, meta, emb_ln_g, emb_ln_b, w_in, rwkv_mu, w0, w_up, a0, a_up, g_up, k_k, k_a, r_k, lnx_g, lnx_b, p_sb, p_rwkv, w_out, ln1_g, ln1_b, router_w, router_b, exp_w_gate, exp_w_up, exp_w_down, ln2_g, ln2_b):
    batch, seq, d = x.shape
    n_meta = meta.shape[0]
    l = seq + n_meta
    lpad = -(-l // SEQ_ALIGN) * SEQ_ALIGN
    depth = w_in.shape[0]

    meta_b = jnp.broadcast_to(meta.astype(x.dtype)[None], (batch, n_meta, d))
    hcat = jnp.concatenate([meta_b, x, jnp.zeros((batch, lpad - l, d), x.dtype)], axis=1)
    h = _embed_ln(hcat.reshape(batch * lpad, d), emb_ln_g.reshape(1, d), emb_ln_b.reshape(1, d))

    idx = jnp.arange(MIX_WIDTH) // HEAD_DIM
    shared = {
        "bd512": (idx[:, None] == idx[None, :]).astype(BF16),
        "router_w": router_w,
        "wg": exp_w_gate.astype(BF16), "wu": exp_w_up.astype(BF16), "wd": exp_w_down.astype(BF16),
        "router_b": router_b.reshape(1, -1),
    }
    qkv_w = 3 * MIX_WIDTH
    for li in range(depth):
        row = lambda a: a[li].reshape(1, -1)
        lp = {
            "w_qkv": w_in[li, :, :qkv_w].astype(BF16),
            "w_rw": w_in[li, :, qkv_w:qkv_w + RW_WIDTH].astype(BF16),
            "w_gates": w_in[li, :, qkv_w + RW_WIDTH:].astype(BF16),
            "mu": row(rwkv_mu), "w0": row(w0), "a0": row(a0), "k_k": row(k_k), "k_a": row(k_a),
            "w_up": w_up[li], "a_up": a_up[li], "g_up": g_up[li],
            "r_k": row(r_k), "lnx_g": row(lnx_g), "lnx_b": row(lnx_b),
            "p_sb": p_sb[li].astype(BF16), "p_rwkv": p_rwkv[li].astype(BF16),
            "w_out": w_out[li].astype(BF16),
            "ln1_g": row(ln1_g), "ln1_b": row(ln1_b), "ln2_g": row(ln2_g), "ln2_b": row(ln2_b),
            "li": li,
        }
        h = _layer(h, lp, shared, batch)
    return h.reshape(batch, lpad, d)[:, n_meta:l]
```

```python
import functools
import math

import jax
import jax.numpy as jnp
from jax import lax
from jax.experimental import pallas as pl
from jax.experimental.pallas import tpu as pltpu

F32 = jnp.float32
BF16 = jnp.bfloat16

D_MODEL = 1024
DEPTH = 2
N_META = 16
HEAD_DIM = 64
PAIR = 2 * HEAD_DIM
MIX_WIDTH = 512
W_LORA, A_LORA, G_LORA = 64, 64, 128
RW_WIDTH = 3 * MIX_WIDTH + W_LORA + A_LORA + G_LORA
DECAY_SCALE = math.exp(-0.5)
LNX_EPS = 64e-5
LN_EPS = 1e-5
N_EXPERTS = 16
N_GROUPS = 4
GROUP = N_EXPERTS // N_GROUPS
D_EXPERT = 512
ALPHA = (2 * DEPTH) ** 0.25

SEQ_ALIGN = 640
ROW_TILE = 640
ATT_BLOCK = 128
ATT_SUB = 32
ATT_PAIRS = 4
SCAN_CHUNK = 64
SCAN_CHUNKS_PER_STEP = 2
EXP_ZERO_BELOW = -104.0
MOE_ROW_TILE = 1280
MOE_CHUNK = 192
MOE_SORT_BLOCK = 256
MOE_EXPERTS_PER_STEP = 2
MOE_ALIGN = 16
VMEM_LIMIT = 48 * 1024 * 1024
MOE_VMEM_LIMIT = 58 * 1024 * 1024


def _cparams(sem):
    return pltpu.CompilerParams(dimension_semantics=sem, vmem_limit_bytes=VMEM_LIMIT)


def _split2(x):
    hi = x.astype(BF16)
    lo = (x - hi.astype(F32)).astype(BF16)
    return hi, lo


def _split3(x):
    hi = x.astype(BF16)
    r1 = x - hi.astype(F32)
    mid = r1.astype(BF16)
    lo = (r1 - mid.astype(F32)).astype(BF16)
    return hi, mid, lo


_NN = (((1,), (0,)), ((), ()))
_NT = (((1,), (1,)), ((), ()))


def _dg(a, b, dims):
    return lax.dot_general(a, b, dims, preferred_element_type=F32)


def _mm(a, b, dims=_NN):
    ah, al = _split2(a)
    bh, bl = _split2(b)
    return _dg(ah, bh, dims) + (_dg(ah, bl, dims) + _dg(al, bh, dims))


def _mm_exact_lhs(a_bf16, b, dims=_NN):
    b1, b2, b3 = _split3(b)
    return _dg(a_bf16, b1, dims) + (_dg(a_bf16, b2, dims) + _dg(a_bf16, b3, dims))


def _mm_exact_rhs(a, b_bf16, dims=_NN):
    a1, a2 = _split2(a)
    return _dg(a1, b_bf16, dims) + _dg(a2, b_bf16, dims)


def _sigmoid(x):
    return 1.0 / (1.0 + jnp.exp(-x))


def _layer_norm(x, g, b):
    mu = jnp.mean(x, axis=-1, keepdims=True)
    xc = x - mu
    var = jnp.mean(xc * xc, axis=-1, keepdims=True)
    return xc * lax.rsqrt(var + LN_EPS) * g + b


def _embed_ln_kernel(x_ref, tail_ref, meta_ref, g_ref, b_ref, o_ref, *, n_real):
    i = pl.program_id(1)
    rows, n_meta = o_ref.shape[1], meta_ref.shape[0]
    head = jnp.where(i == 0, meta_ref[...], tail_ref[0])
    tile = jnp.concatenate([head, x_ref[0, :rows - n_meta, :]], axis=0)
    pos = i * rows + lax.broadcasted_iota(jnp.int32, (rows, 1), 0)
    tile = jnp.where(pos < n_real, tile, 0.0)
    o_ref[0] = _layer_norm(tile, g_ref[...], b_ref[...])


def _embed_ln(x, meta, g, b, lpad):
    bsz, seq, d = x.shape
    n_meta = meta.shape[0]
    rows = ROW_TILE
    per = rows // n_meta
    return pl.pallas_call(
        functools.partial(_embed_ln_kernel, n_real=seq + n_meta),
        out_shape=jax.ShapeDtypeStruct((bsz, lpad, d), F32),
        grid=(bsz, lpad // rows),
        in_specs=[pl.BlockSpec((1, rows, d), lambda bi, i: (bi, i, 0)),
                  pl.BlockSpec((1, n_meta, d), lambda bi, i: (bi, jnp.maximum(i * per - 1, 0), 0)),
                  pl.BlockSpec((n_meta, d), lambda bi, i: (0, 0)),
                  pl.BlockSpec((1, d), lambda bi, i: (0, 0)),
                  pl.BlockSpec((1, d), lambda bi, i: (0, 0))],
        out_specs=pl.BlockSpec((1, rows, d), lambda bi, i: (bi, i, 0)),
        compiler_params=_cparams(("parallel", "parallel")),
        name="embed_ln",
    )(x, x, meta, g, b)


def _sb_attn_kernel(q_ref, k_ref, v_ref, o_ref, acc_sc, c_sc):
    blk = ATT_BLOCK
    sub = ATT_SUB
    npair = q_ref.shape[2] // PAIR
    i = pl.program_id(2)
    first = lax.broadcasted_iota(jnp.int32, (1, PAIR), 1) < HEAD_DIM
    heads = []
    for p in range(npair):
        q = q_ref[0, :, p * PAIR:(p + 1) * PAIR]
        zero = jnp.zeros_like(q)
        heads += [(p, jnp.where(first, q, zero)), (p, jnp.where(first, zero, q))]
    rr = lax.broadcasted_iota(jnp.int32, (blk, blk), 0)
    cc = lax.broadcasted_iota(jnp.int32, (blk, blk), 1)
    causal = cc < rr
    later = (rr > cc).astype(BF16)

    def sweep(j, r0, nrows, diagonal):
        rows = slice(r0, r0 + nrows)
        start = pl.multiple_of(j * blk, blk)
        kbs = [k_ref[0, pl.ds(start, blk), p * PAIR:(p + 1) * PAIR] for p in range(npair)]
        vbs = [v_ref[0, pl.ds(start, blk), p * PAIR:(p + 1) * PAIR] for p in range(npair)]
        zs = [_dg(qh[rows], kbs[p], _NT) for p, qh in heads]
        sps = [jnp.maximum(z, 0.0) + jnp.log(1.0 + jnp.exp(-jnp.abs(z))) for z in zs]
        if diagonal:
            us = [jnp.where(causal[rows], sp, 0.0) for sp in sps]
            cs = [jnp.zeros((nrows, 1), F32)] * len(heads)
        else:
            us = sps
            cs = [c_sc[h, rows, :] for h in range(len(heads))]
        rests = [c - _mm_exact_rhs(u, later) for u, c in zip(us, cs)]
        ws = [jnp.exp((z - sp) + rest) for z, sp, rest in zip(zs, sps, rests)]
        if diagonal:
            ws = [jnp.where(causal[rows], a, 0.0) for a in ws]
        pvs = [jnp.dot(a.astype(BF16), vbs[p], preferred_element_type=F32) for (p, _), a in zip(heads, ws)]
        for p in range(npair):
            pv = jnp.where(first, pvs[2 * p], pvs[2 * p + 1])
            if diagonal:
                acc_sc[rows, p * PAIR:(p + 1) * PAIR] = pv
            else:
                acc_sc[rows, p * PAIR:(p + 1) * PAIR] += pv
        cs = [c - jnp.sum(u, axis=1, keepdims=True) for c, u in zip(cs, us)]
        for h, c in enumerate(cs):
            c_sc[h, rows, :] = c
        return cs

    def live(cs, r0, nrows):
        m = cs[0][r0:r0 + nrows]
        for c in cs[1:]:
            m = jnp.maximum(m, c[r0:r0 + nrows])
        return (jnp.max(m) > EXP_ZERO_BELOW).astype(jnp.int32)

    nsub = blk // sub
    cs = sweep(i, 0, blk, True)
    alive0 = tuple(live(cs, g * sub, sub) for g in range(nsub))

    def cond(st):
        j, alive = st
        any_alive = alive[0]
        for a in alive[1:]:
            any_alive = jnp.maximum(any_alive, a)
        return jnp.logical_and(j >= 0, any_alive > 0)

    def body(st):
        j, alive = st
        all_alive = alive[0]
        for a in alive[1:]:
            all_alive = jnp.minimum(all_alive, a)

        def whole():
            cs = sweep(j, 0, blk, False)
            return tuple(live(cs, g * sub, sub) for g in range(nsub))

        def by_group():
            return tuple(lax.cond(alive[g] > 0,
                                  lambda g=g: live(sweep(j, g * sub, sub, False), 0, sub),
                                  lambda: jnp.int32(0)) for g in range(nsub))

        return j - 1, lax.cond(all_alive > 0, whole, by_group)

    lax.while_loop(cond, body, (i - 1, alive0))
    o_ref[0] = acc_sc[...].astype(o_ref.dtype)


def _sb_attention(q, k, v):
    b, l, w = q.shape
    wb = ATT_PAIRS * PAIR
    return pl.pallas_call(
        _sb_attn_kernel,
        out_shape=jax.ShapeDtypeStruct((b, l, w), BF16),
        grid=(b, w // wb, l // ATT_BLOCK),
        in_specs=[pl.BlockSpec((1, ATT_BLOCK, wb), lambda bi, p, i: (bi, i, p)),
                  pl.BlockSpec((1, l, wb), lambda bi, p, i: (bi, 0, p)),
                  pl.BlockSpec((1, l, wb), lambda bi, p, i: (bi, 0, p))],
        out_specs=pl.BlockSpec((1, ATT_BLOCK, wb), lambda bi, p, i: (bi, i, p)),
        scratch_shapes=[pltpu.VMEM((ATT_BLOCK, wb), F32),
                        pltpu.VMEM((wb // HEAD_DIM, ATT_BLOCK, 1), F32)],
        compiler_params=_cparams(("parallel", "parallel", "arbitrary")),
        name="sb_attention",
    )(q, k, v)


def _in_proj_kernel(h_ref, w_ref, mu_ref, w0_ref, wup_ref, a0_ref, aup_ref, gup_ref,
                    kk_ref, ka_ref, bd_ref,
                    q_o, ka_o, va_o, r_o, lw_o, k_o, v_o, kn_o, b_o, g_o, prev_sc):
    t = pl.program_id(1)
    x = h_ref[0].astype(BF16)
    pall = jnp.dot(x, w_ref[...], preferred_element_type=F32)
    q_o[0] = (pall[:, :MIX_WIDTH] * (HEAD_DIM ** -0.5)).astype(BF16)
    ka_o[0] = pall[:, MIX_WIDTH:2 * MIX_WIDTH].astype(BF16)
    va_o[0] = pall[:, 2 * MIX_WIDTH:3 * MIX_WIDTH].astype(BF16)
    p = pall[:, 3 * MIX_WIDTH:]
    rows = p.shape[0]

    @pl.when(t == 0)
    def _():
        prev_sc[...] = jnp.zeros_like(prev_sc)

    row = lax.broadcasted_iota(jnp.int32, (rows, 1), 0)
    shifted = jnp.where(row == 0, prev_sc[...], pltpu.roll(p, 1, axis=0))
    prev_sc[...] = p[rows - 1:rows, :]
    xs = p + (shifted - p) * mu_ref[...]

    w = MIX_WIDTH
    r, k, v = xs[:, :w], xs[:, w:2 * w], xs[:, 2 * w:3 * w]
    wd = xs[:, 3 * w:3 * w + W_LORA]
    ad = xs[:, 3 * w + W_LORA:3 * w + W_LORA + A_LORA]
    gd = xs[:, 3 * w + W_LORA + A_LORA:]

    def lora(x, up_ref):
        return jnp.dot(x.astype(BF16), up_ref[...].astype(BF16), preferred_element_type=F32)

    lw = -DECAY_SCALE * _sigmoid(w0_ref[...] + lora(jnp.tanh(wd), wup_ref))
    a = _sigmoid(a0_ref[...] + lora(ad, aup_ref))
    g = lora(_sigmoid(gd), gup_ref)

    kk = k * kk_ref[...]
    ssq = jnp.dot((kk * kk).astype(BF16), bd_ref[...], preferred_element_type=F32)
    kn = kk / jnp.maximum(jnp.sqrt(ssq), 1e-12)
    r_o[0] = r
    lw_o[0] = lw
    k_o[0] = k * (1.0 + (a - 1.0) * ka_ref[...])
    v_o[0] = v
    kn_o[0] = kn
    b_o[0] = kn * a
    g_o[0] = g


def _in_proj(h3, w_qkv_rw, mu, w0, w_up, a0, a_up, g_up, k_k, k_a, bd512):
    b, l, d = h3.shape
    nt = l // ROW_TILE
    out = jax.ShapeDtypeStruct((b, l, MIX_WIDTH), F32)
    out16 = jax.ShapeDtypeStruct((b, l, MIX_WIDTH), BF16)
    row = pl.BlockSpec((1, ROW_TILE, MIX_WIDTH), lambda bi, t: (bi, t, 0))

    def full(arr):
        return pl.BlockSpec(arr.shape, lambda bi, t: (0,) * arr.ndim)

    consts = (w_qkv_rw, mu, w0, w_up, a0, a_up, g_up, k_k, k_a, bd512)
    return pl.pallas_call(
        _in_proj_kernel,
        out_shape=(out16,) * 3 + (out,) * 7,
        grid=(b, nt),
        in_specs=[pl.BlockSpec((1, ROW_TILE, d), lambda bi, t: (bi, t, 0))] + [full(c) for c in consts],
        out_specs=(row,) * 10,
        scratch_shapes=[pltpu.VMEM((1, RW_WIDTH), F32)],
        compiler_params=pltpu.CompilerParams(dimension_semantics=("parallel", "arbitrary"),
                                             vmem_limit_bytes=MOE_VMEM_LIMIT),
        name="in_proj",
    )(h3, *consts)


def _rwkv_scan_kernel(r_ref, lw_ref, k_ref, v_ref, kn_ref, b_ref, g_ref,
                      rk_ref, lng_ref, lnb_ref, o_ref, state_sc):
    nb, rows, w = r_ref.shape
    c = SCAN_CHUNK
    nchunk = rows // c
    npair = w // PAIR
    assert 2 * c == PAIR

    @pl.when(pl.program_id(0) == 0)
    def _():
        state_sc[...] = jnp.zeros_like(state_sc)

    rr = lax.broadcasted_iota(jnp.int32, (c, c), 0)
    cc = lax.broadcasted_iota(jnp.int32, (c, c), 1)
    ones_incl = (rr >= cc).astype(BF16)
    row_c = lax.broadcasted_iota(jnp.int32, (c, PAIR), 0)
    key_c = lax.broadcasted_iota(jnp.int32, (c, PAIR), 1) & (HEAD_DIM - 1)
    strict_m = key_c < row_c
    incl_m = key_c <= row_c
    first = lax.broadcasted_iota(jnp.int32, (1, PAIR), 1) < HEAD_DIM
    pr = lax.broadcasted_iota(jnp.int32, (PAIR, PAIR), 0)
    pc = lax.broadcasted_iota(jnp.int32, (PAIR, PAIR), 1)
    same_head = (pr < HEAD_DIM) == (pc < HEAD_DIM)
    head0_rows = lax.broadcasted_iota(jnp.int32, (PAIR, 2 * PAIR), 0) < HEAD_DIM
    head_ones = same_head.astype(BF16)
    eye_p = pr == pc
    eye_sbs = (key_c == row_c).astype(F32)
    zeros_c = jnp.zeros((c, PAIR), F32)

    def cat0(*xs):
        return jnp.concatenate(xs, axis=0)

    def cat1(*xs):
        return jnp.concatenate(xs, axis=1)

    def mm(a, b, dims=_NN):
        return _dg(a.astype(BF16), b.astype(BF16), dims)

    streams = [(bi, p, ci) for bi in range(nb) for p in range(npair) for ci in range(nchunk)]

    def sl_of(s):
        bi, p, ci = s
        return (bi, slice(ci * c, (ci + 1) * c), slice(p * PAIR, (p + 1) * PAIR))

    pre = {}
    for bi in range(nb):
        for ci in range(nchunk):
            rs = (bi, slice(ci * c, (ci + 1) * c), slice(None))
            lw = lw_ref[rs]
            cl = _mm_exact_lhs(ones_incl, lw)
            cl_end = cl[c - 1:c, :]
            e_pos = jnp.exp(cl)
            e_neg = jnp.exp(-cl)
            e_end = jnp.exp(cl_end - cl)
            r, k, kn, b = r_ref[rs], k_ref[rs], kn_ref[rs], b_ref[rs]
            full = dict(a_t=-kn * jnp.exp(cl - lw), r_t=r * e_pos, b_t=b * e_neg, k_t=k * e_neg,
                        b_h=b * e_end, k_h=k * e_end, g_end=jnp.exp(cl_end))
            for p in range(npair):
                pre[(bi, p, ci)] = {n: x[:, p * PAIR:(p + 1) * PAIR] for n, x in full.items()}

    tri = {}
    for s in streams:
        d = pre[s]
        a_t, r_t, b_t, k_t = d["a_t"], d["r_t"], d["b_t"], d["k_t"]
        bk = cat0(b_t, k_t)
        s0 = mm(jnp.where(first, cat0(a_t, r_t), 0.0), bk, _NT)
        s1t = mm(jnp.where(first, 0.0, a_t), cat0(k_t, b_t), _NT)
        s1b = mm(jnp.where(first, 0.0, r_t), bk, _NT)
        top0 = jnp.where(strict_m, s0[:c], 0.0)
        top1 = jnp.where(strict_m, s1t, 0.0)
        tri[s] = dict(
            l_ab_sbs=jnp.where(first, top0, top1),
            l_ak=cat0(jnp.where(first, 0.0, top0), jnp.where(first, top1, 0.0)),
            m0=jnp.where(incl_m, s0[c:], 0.0), m1=jnp.where(incl_m, s1b, 0.0))

    xs = {}
    for s in streams:
        v = v_ref[sl_of(s)]
        xs[s] = mm(tri[s]["l_ak"], cat0(v, v))

    def diag2(x):
        return cat0(jnp.where(first, x, 0.0), jnp.where(first, 0.0, x))

    pw = {s: tri[s]["l_ab_sbs"] for s in streams}
    tinv = {s: eye_sbs + pw[s] for s in streams}
    sq = {s: mm(pw[s], diag2(pw[s])) for s in streams}
    n = 2
    while 2 * n < c:
        for s in streams:
            both = mm(cat0(sq[s], tinv[s]), diag2(sq[s]))
            tinv[s] = tinv[s] + both[c:]
            sq[s] = both[:c]
        n *= 2
    for s in streams:
        tinv[s] = diag2(tinv[s] + mm(tinv[s], diag2(sq[s])))

    pq = {}
    for s in streams:
        a_t = pre[s]["a_t"]
        rhs = cat0(cat1(a_t, xs[s][:c]), cat1(a_t, xs[s][c:]))
        pq[s] = mm(tinv[s], rhs)

    coef = {}
    for s in streams:
        d = pre[s]
        v = v_ref[sl_of(s)]
        bkh_t = cat0(d["b_h"], d["k_h"]).T
        zv = cat1(zeros_c, v)
        res0 = mm(cat0(tri[s]["m0"], bkh_t), cat0(pq[s][:c], zv))
        res1 = mm(cat0(tri[s]["m1"], bkh_t), cat0(pq[s][c:], zv))
        ry = d["r_t"] + jnp.where(first, res0[:c, :PAIR], res1[:c, :PAIR])
        y0 = jnp.where(first, res0[:c, PAIR:], res1[:c, PAIR:])
        gj = jnp.where(head0_rows, res0[c:], res1[c:])
        gmat = jnp.where(same_head, gj[:, :PAIR], 0.0) + jnp.where(eye_p, d["g_end"], 0.0)
        jmat = jnp.where(same_head, gj[:, PAIR:], 0.0)
        coef[s] = (cat0(ry, gmat), y0, jmat)

    ys = {}
    for bi in range(nb):
        for p in range(npair):
            idx = bi * npair + p
            state = state_sc[idx]
            for ci in range(nchunk):
                lhs, y0, jmat = coef[(bi, p, ci)]
                st = mm(lhs, state)
                ys[(bi, p, ci)] = st[:c] + y0
                state = st[c:] + jmat
            state_sc[idx] = state

    for s in streams:
        sl = sl_of(s)
        csl = (slice(None), sl[2])
        y = ys[s]
        v = v_ref[sl]
        r, k = r_ref[sl], k_ref[sl]
        mean = mm(y, head_ones) * (1.0 / HEAD_DIM)
        yc = y - mean
        var = mm(yc * yc, head_ones) * (1.0 / HEAD_DIM)
        yn = yc * lax.rsqrt(var + LNX_EPS) * lng_ref[csl] + lnb_ref[csl]
        bonus = mm(r * k * rk_ref[csl], head_ones) * v
        o_ref[sl] = ((yn + bonus) * g_ref[sl]).astype(o_ref.dtype)


def _rwkv_scan(r, lw, k, v, kn, bb, g, r_k, lnx_g, lnx_b):
    b, l, w = r.shape
    c = SCAN_CHUNK * SCAN_CHUNKS_PER_STEP
    row = pl.BlockSpec((b, c, w), lambda i: (0, i, 0))
    vec = pl.BlockSpec((1, w), lambda i: (0, 0))
    return pl.pallas_call(
        _rwkv_scan_kernel,
        out_shape=jax.ShapeDtypeStruct((b, l, w), BF16),
        grid=(l // c,),
        in_specs=[row] * 7 + [vec] * 3,
        out_specs=row,
        scratch_shapes=[pltpu.VMEM((b * (w // PAIR), PAIR, PAIR), F32)],
        compiler_params=_cparams(("arbitrary",)),
        name="rwkv_scan",
    )(r, lw, k, v, kn, bb, g, r_k, lnx_g, lnx_b)


def _merge_kernel(h_ref, osb_ref, orw_ref, wg_ref, psb_ref, prw_ref, wout_ref,
                  g1_ref, b1_ref, rw_ref, rb_ref, h1_ref, comb_ref):
    h = h_ref[...]
    hb = h.astype(BF16)
    gates = _sigmoid(jnp.dot(hb, wg_ref[...], preferred_element_type=F32))
    sb = jnp.dot(osb_ref[...], psb_ref[...], preferred_element_type=F32)
    rw = jnp.dot(orw_ref[...], prw_ref[...], preferred_element_type=F32)
    merged = gates[:, :D_MODEL] * sb + gates[:, D_MODEL:] * rw
    mix = jnp.dot(merged.astype(BF16), wout_ref[...], preferred_element_type=F32)
    h1 = _layer_norm(ALPHA * h + mix, g1_ref[...], b1_ref[...])
    h1_ref[...] = h1

    logits = _mm(h1, rw_ref[...]) + rb_ref[...]
    m = jnp.max(logits, axis=-1, keepdims=True)
    ex = jnp.exp(logits - m)
    probs = ex / jnp.sum(ex, axis=-1, keepdims=True)
    lane = lax.broadcasted_iota(jnp.int32, probs.shape, 1)
    big = N_EXPERTS

    def top2(vals):
        v1 = jnp.max(vals, axis=-1, keepdims=True)
        i1 = jnp.min(jnp.where(vals == v1, lane, big), axis=-1, keepdims=True)
        rest = jnp.where(lane == i1, -2.0, vals)
        v2 = jnp.max(rest, axis=-1, keepdims=True)
        i2 = jnp.min(jnp.where(rest == v2, lane, big), axis=-1, keepdims=True)
        return v1, i1, v2, i2

    best = None
    sel = None
    for gi in range(N_GROUPS):
        in_g = (lane >= gi * GROUP) & (lane < (gi + 1) * GROUP)
        v1, _, v2, _ = top2(jnp.where(in_g, probs, -1.0))
        score = v1 + v2
        if gi == 0:
            best, sel = score, jnp.zeros_like(score, dtype=jnp.int32)
        else:
            better = score > best
            best = jnp.where(better, score, best)
            sel = jnp.where(better, gi, sel)
    in_sel = (lane >= sel * GROUP) & (lane < (sel + 1) * GROUP)
    v1, i1, v2, i2 = top2(jnp.where(in_sel, probs, -1.0))
    denom = v1 + v2
    comb_ref[...] = jnp.where(lane == i1, v1 / denom, 0.0) + jnp.where(lane == i2, v2 / denom, 0.0)


def _merge(h, o_sb, o_rw, w_gates, p_sb, p_rwkv, w_out, ln_g, ln_b, router_w, router_b):
    t, d = h.shape

    def full(arr):
        return pl.BlockSpec(arr.shape, lambda i: (0,) * arr.ndim)

    consts = (w_gates, p_sb, p_rwkv, w_out, ln_g, ln_b, router_w, router_b)
    return pl.pallas_call(
        _merge_kernel,
        out_shape=(jax.ShapeDtypeStruct((t, d), F32), jax.ShapeDtypeStruct((t, N_EXPERTS), F32)),
        grid=(t // ROW_TILE,),
        in_specs=[pl.BlockSpec((ROW_TILE, d), lambda i: (i, 0)),
                  pl.BlockSpec((ROW_TILE, MIX_WIDTH), lambda i: (i, 0)),
                  pl.BlockSpec((ROW_TILE, MIX_WIDTH), lambda i: (i, 0))] + [full(c) for c in consts],
        out_specs=(pl.BlockSpec((ROW_TILE, d), lambda i: (i, 0)),
                   pl.BlockSpec((ROW_TILE, N_EXPERTS), lambda i: (i, 0))),
        compiler_params=_cparams(("parallel",)),
        name="merge_ln1_router",
    )(h, o_sb, o_rw, *consts)


def _moe_kernel(h_ref, comb_ref, wg_ref, wu_ref, wd_ref, g2_ref, b2_ref, o_ref,
                xs_sc, ys_sc, col_sc, seg_sm):
    e = pl.program_id(1)
    tm, d = h_ref.shape
    ne = comb_ref.shape[1]
    blk = MOE_SORT_BLOCK
    chunk = MOE_CHUNK
    ns = xs_sc.shape[0] - chunk

    @pl.when(e == 0)
    def _dispatch():
        comb = comb_ref[...]
        sel = comb != 0.0
        self32 = jnp.where(sel, 1.0, 0.0)
        selb = self32.astype(BF16)
        lane = lax.broadcasted_iota(jnp.int32, (tm, ne), 1)
        rr = lax.broadcasted_iota(jnp.int32, (blk, blk), 0)
        cc = lax.broadcasted_iota(jnp.int32, (blk, blk), 1)
        before = jnp.where(cc < rr, 1.0, 0.0).astype(BF16)
        ranks = []
        seen = jnp.zeros((1, ne), F32)
        for r0 in range(0, tm, blk):
            ranks.append(jnp.dot(before, selb[r0:r0 + blk], preferred_element_type=F32) + seen)
            seen = seen + jnp.sum(self32[r0:r0 + blk], axis=0, keepdims=True)
        rank = jnp.concatenate(ranks, axis=0)
        counts = jnp.sum(self32, axis=0, keepdims=True).astype(jnp.int32)
        lane1 = lax.broadcasted_iota(jnp.int32, (1, ne), 1)
        off = jnp.int32(0)
        offv = jnp.zeros((1, ne), F32)
        for ex in range(ne):
            n = counts[0, ex]
            seg_sm[0, ex] = off
            seg_sm[1, ex] = n
            offv = jnp.where(lane1 == ex, off.astype(F32), offv)
            off = off + ((n + (MOE_ALIGN - 1)) // MOE_ALIGN) * MOE_ALIGN
        pos = offv + rank

        def pick(idx, arr):
            return jnp.sum(jnp.where(lane == idx, arr, 0.0), axis=1, keepdims=True)

        i_lo = jnp.min(jnp.where(sel, lane, ne), axis=1, keepdims=True)
        i_hi = jnp.max(jnp.where(sel, lane, -1), axis=1, keepdims=True)
        any_sel = i_hi >= 0
        two = i_hi != i_lo
        p_lo = jnp.where(any_sel, pick(i_lo, pos), -1.0)
        p_hi = jnp.where(two, pick(i_hi, pos), -1.0)
        g_lo = pick(i_lo, comb)
        g_hi = jnp.where(two, pick(i_hi, comb), 0.0)
        l128 = lax.broadcasted_iota(jnp.int32, (tm, 128), 1)
        cols = jnp.where(l128 == 0, p_lo, jnp.where(l128 == 1, p_hi, jnp.where(l128 == 2, g_lo,
                         jnp.where(l128 == 3, g_hi, 0.0))))
        col_sc[...] = cols
        rows_t = cols.T
        p_lo_r, p_hi_r = rows_t[0:1, :], rows_t[1:2, :]
        x = h_ref[...].astype(BF16)
        for r0 in range(0, ns, blk):
            ri = (lax.broadcasted_iota(jnp.int32, (blk, tm), 0) + r0).astype(F32)
            onehot = jnp.where((ri == p_lo_r) | (ri == p_hi_r), 1.0, 0.0).astype(BF16)
            xs_sc[r0:r0 + blk, :] = jnp.dot(onehot, x, preferred_element_type=F32).astype(BF16)
        xs_sc[ns:, :] = jnp.zeros((chunk, d), BF16)
        ys_sc[...] = jnp.zeros_like(ys_sc)

    for sub in range(wg_ref.shape[1]):
        ex = e * wg_ref.shape[1] + sub
        seg_off = seg_sm[0, ex]
        seg_n = seg_sm[1, ex]

        def ffn_chunk(c, carry, sub=sub, seg_off=seg_off):
            start = pl.multiple_of(seg_off + c * chunk, MOE_ALIGN)
            x = xs_sc[pl.ds(start, chunk), :]
            gate = jnp.dot(x, wg_ref[0, sub], preferred_element_type=F32)
            up = jnp.dot(x, wu_ref[0, sub], preferred_element_type=F32)
            hid = gate * _sigmoid(gate) * up
            ys_sc[pl.ds(start, chunk), :] = jnp.dot(hid.astype(BF16), wd_ref[0, sub],
                                                    preferred_element_type=F32).astype(BF16)
            return carry

        lax.fori_loop(0, (seg_n + (chunk - 1)) // chunk, ffn_chunk, 0)

    @pl.when(e == pl.num_programs(1) - 1)
    def _combine():
        ys = ys_sc[0:ns, :]
        for r0 in range(0, tm, blk):
            cols = col_sc[r0:r0 + blk, :]
            p_lo, p_hi, g_lo, g_hi = cols[:, 0:1], cols[:, 1:2], cols[:, 2:3], cols[:, 3:4]
            li = lax.broadcasted_iota(jnp.int32, (blk, ns), 1).astype(F32)
            wsel = jnp.where(li == p_lo, g_lo, 0.0) + jnp.where(li == p_hi, g_hi, 0.0)
            ff = jnp.dot(wsel.astype(BF16), ys, preferred_element_type=F32)
            o_ref[r0:r0 + blk, :] = _layer_norm(ALPHA * h_ref[r0:r0 + blk, :] + ff, g2_ref[...], b2_ref[...])


def _moe(h, comb, wg, wu, wd, li, ln_g, ln_b):
    t, d = h.shape
    _, ne, _, f = wg.shape
    rows = MOE_ROW_TILE
    eps = MOE_EXPERTS_PER_STEP
    ns = -(-(2 * rows + ne * (MOE_ALIGN - 1)) // MOE_SORT_BLOCK) * MOE_SORT_BLOCK
    return pl.pallas_call(
        _moe_kernel,
        out_shape=jax.ShapeDtypeStruct((t, d), F32),
        grid=(t // rows, ne // eps),
        in_specs=[pl.BlockSpec((rows, d), lambda i, e: (i, 0), pipeline_mode=pl.Buffered(1)),
                  pl.BlockSpec((rows, ne), lambda i, e: (i, 0)),
                  pl.BlockSpec((1, eps, d, f), lambda i, e: (li, e, 0, 0)),
                  pl.BlockSpec((1, eps, d, f), lambda i, e: (li, e, 0, 0)),
                  pl.BlockSpec((1, eps, f, d), lambda i, e: (li, e, 0, 0)),
                  pl.BlockSpec((1, d), lambda i, e: (0, 0)),
                  pl.BlockSpec((1, d), lambda i, e: (0, 0))],
        out_specs=pl.BlockSpec((rows, d), lambda i, e: (i, 0)),
        scratch_shapes=[pltpu.VMEM((ns + MOE_CHUNK, d), BF16),
                        pltpu.VMEM((ns + MOE_CHUNK, d), BF16),
                        pltpu.VMEM((rows, 128), F32),
                        pltpu.SMEM((2, ne), jnp.int32)],
        compiler_params=pltpu.CompilerParams(dimension_semantics=("parallel", "arbitrary"),
                                             vmem_limit_bytes=MOE_VMEM_LIMIT),
        name="moe_ln2",
    )(h, comb, wg, wu, wd, ln_g, ln_b)


def _layer(h, lp, shared, batch):
    t, d = h.shape
    l = t // batch
    proj = _in_proj(h.reshape(batch, l, d), lp["w_qkv_rw"], lp["mu"], lp["w0"], lp["w_up"], lp["a0"],
                    lp["a_up"], lp["g_up"], lp["k_k"], lp["k_a"], shared["bd512"])
    o_sb = _sb_attention(*proj[:3]).reshape(t, MIX_WIDTH)
    o_rw = _rwkv_scan(*proj[3:], lp["r_k"], lp["lnx_g"], lp["lnx_b"]).reshape(t, MIX_WIDTH)
    h1, comb = _merge(h, o_sb, o_rw, lp["w_gates"], lp["p_sb"], lp["p_rwkv"], lp["w_out"],
                      lp["ln1_g"], lp["ln1_b"], shared["router_w"], shared["router_b"])
    return _moe(h1, comb, shared["wg"], shared["wu"], shared["wd"], lp["li"], lp["ln2_g"], lp["ln2_b"])


def kernel(x, meta, emb_ln_g, emb_ln_b, w_in, rwkv_mu, w0, w_up, a0, a_up, g_up, k_k, k_a, r_k, lnx_g, lnx_b, p_sb, p_rwkv, w_out, ln1_g, ln1_b, router_w, router_b, exp_w_gate, exp_w_up, exp_w_down, ln2_g, ln2_b):
    batch, seq, d = x.shape
    n_meta = meta.shape[0]
    l = seq + n_meta
    lpad = -(-l // SEQ_ALIGN) * SEQ_ALIGN
    depth = w_in.shape[0]

    h = _embed_ln(x, meta.astype(x.dtype), emb_ln_g.reshape(1, d), emb_ln_b.reshape(1, d), lpad)
    h = h.reshape(batch * lpad, d)

    idx = jnp.arange(MIX_WIDTH) // HEAD_DIM
    shared = {
        "bd512": (idx[:, None] == idx[None, :]).astype(BF16),
        "router_w": router_w,
        "wg": exp_w_gate.astype(BF16), "wu": exp_w_up.astype(BF16), "wd": exp_w_down.astype(BF16),
        "router_b": router_b.reshape(1, -1),
    }
    qkv_w = 3 * MIX_WIDTH
    for li in range(depth):
        row = lambda a: a[li].reshape(1, -1)
        lp = {
            "w_qkv_rw": w_in[li, :, :qkv_w + RW_WIDTH].astype(BF16),
            "w_gates": w_in[li, :, qkv_w + RW_WIDTH:].astype(BF16),
            "mu": row(rwkv_mu), "w0": row(w0), "a0": row(a0), "k_k": row(k_k), "k_a": row(k_a),
            "w_up": w_up[li], "a_up": a_up[li], "g_up": g_up[li],
            "r_k": row(r_k), "lnx_g": row(lnx_g), "lnx_b": row(lnx_b),
            "p_sb": p_sb[li].astype(BF16), "p_rwkv": p_rwkv[li].astype(BF16),
            "w_out": w_out[li].astype(BF16),
            "ln1_g": row(ln1_g), "ln1_b": row(ln1_b), "ln2_g": row(ln2_g), "ln2_b": row(ln2_b),
            "li": li,
        }
        h = _layer(h, lp, shared, batch)
    return h.reshape(batch, lpad, d)[:, n_meta:l]
```

```python
import functools
import math

import jax
import jax.numpy as jnp
from jax import lax
from jax.experimental import pallas as pl
from jax.experimental.pallas import tpu as pltpu

F32 = jnp.float32
BF16 = jnp.bfloat16

D_MODEL = 1024
DEPTH = 2
N_META = 16
HEAD_DIM = 64
PAIR = 2 * HEAD_DIM
LANE_TILE = 128
MIX_WIDTH = 512
W_LORA, A_LORA, G_LORA = 64, 64, 128
RW_WIDTH = 3 * MIX_WIDTH + W_LORA + A_LORA + G_LORA
DECAY_SCALE = math.exp(-0.5)
LNX_EPS = 64e-5
LN_EPS = 1e-5
N_EXPERTS = 16
N_GROUPS = 4
GROUP = N_EXPERTS // N_GROUPS
D_EXPERT = 512
ALPHA = (2 * DEPTH) ** 0.25

SEQ_ALIGN = 640
ROW_TILE = 640
ATT_BLOCK = 128
ATT_SUB = 32
ATT_PAIRS = 4
SCAN_CHUNK = 64
SCAN_CHUNKS_PER_STEP = 2
EXP_ZERO_BELOW = -104.0
MOE_ROW_TILE = 1280
MOE_CHUNK = 192
MOE_SORT_BLOCK = 256
MOE_EXPERTS_PER_STEP = 2
MOE_ALIGN = 16
VMEM_LIMIT = 48 * 1024 * 1024
MOE_VMEM_LIMIT = 58 * 1024 * 1024


def _cparams(sem):
    return pltpu.CompilerParams(dimension_semantics=sem, vmem_limit_bytes=VMEM_LIMIT)


def _split2(x):
    hi = x.astype(BF16)
    lo = (x - hi.astype(F32)).astype(BF16)
    return hi, lo


_NN = (((1,), (0,)), ((), ()))
_NT = (((1,), (1,)), ((), ()))


def _dg(a, b, dims):
    return lax.dot_general(a, b, dims, preferred_element_type=F32)


def _mm(a, b, dims=_NN):
    ah, al = _split2(a)
    bh, bl = _split2(b)
    return _dg(ah, bh, dims) + (_dg(ah, bl, dims) + _dg(al, bh, dims))


def _mm_exact_lhs(a_bf16, b, dims=_NN):
    b1, b2 = _split2(b)
    return _dg(a_bf16, b1, dims) + _dg(a_bf16, b2, dims)


def _mm_exact_rhs(a, b_bf16, dims=_NN):
    a1, a2 = _split2(a)
    return _dg(a1, b_bf16, dims) + _dg(a2, b_bf16, dims)


def _sigmoid(x):
    return 1.0 / (1.0 + jnp.exp(-x))


def _layer_norm(x, g, b):
    mu = jnp.mean(x, axis=-1, keepdims=True)
    xc = x - mu
    var = jnp.mean(xc * xc, axis=-1, keepdims=True)
    return xc * lax.rsqrt(var + LN_EPS) * g + b


def _embed_ln_kernel(x_ref, tail_ref, meta_ref, g_ref, b_ref, o_ref, *, n_real):
    i = pl.program_id(1)
    rows, n_meta = o_ref.shape[1], meta_ref.shape[0]
    head = jnp.where(i == 0, meta_ref[...], tail_ref[0])
    tile = jnp.concatenate([head, x_ref[0, :rows - n_meta, :]], axis=0)
    pos = i * rows + lax.broadcasted_iota(jnp.int32, (rows, 1), 0)
    tile = jnp.where(pos < n_real, tile, 0.0)
    o_ref[0] = _layer_norm(tile, g_ref[...], b_ref[...])


def _embed_ln(x, meta, g, b, lpad):
    bsz, seq, d = x.shape
    n_meta = meta.shape[0]
    rows = ROW_TILE
    per = rows // n_meta
    return pl.pallas_call(
        functools.partial(_embed_ln_kernel, n_real=seq + n_meta),
        out_shape=jax.ShapeDtypeStruct((bsz, lpad, d), F32),
        grid=(bsz, lpad // rows),
        in_specs=[pl.BlockSpec((1, rows, d), lambda bi, i: (bi, i, 0)),
                  pl.BlockSpec((1, n_meta, d), lambda bi, i: (bi, jnp.maximum(i * per - 1, 0), 0)),
                  pl.BlockSpec((n_meta, d), lambda bi, i: (0, 0)),
                  pl.BlockSpec((1, d), lambda bi, i: (0, 0)),
                  pl.BlockSpec((1, d), lambda bi, i: (0, 0))],
        out_specs=pl.BlockSpec((1, rows, d), lambda bi, i: (bi, i, 0)),
        compiler_params=_cparams(("parallel", "parallel")),
        name="embed_ln",
    )(x, x, meta, g, b)


def _sb_attn_kernel(q_ref, k_ref, v_ref, o_ref, acc_sc, c_sc):
    blk = ATT_BLOCK
    sub = ATT_SUB
    npair = q_ref.shape[2] // PAIR
    i = pl.program_id(2)
    first = lax.broadcasted_iota(jnp.int32, (1, PAIR), 1) < HEAD_DIM
    heads = []
    for p in range(npair):
        q = q_ref[0, :, p * PAIR:(p + 1) * PAIR]
        zero = jnp.zeros_like(q)
        heads += [(p, jnp.where(first, q, zero)), (p, jnp.where(first, zero, q))]
    rr = lax.broadcasted_iota(jnp.int32, (blk, blk), 0)
    cc = lax.broadcasted_iota(jnp.int32, (blk, blk), 1)
    causal = cc < rr
    later = (rr > cc).astype(BF16)

    def sweep(j, r0, nrows, diagonal):
        rows = slice(r0, r0 + nrows)
        start = pl.multiple_of(j * blk, blk)
        kbs = [k_ref[0, pl.ds(start, blk), p * PAIR:(p + 1) * PAIR] for p in range(npair)]
        vbs = [v_ref[0, pl.ds(start, blk), p * PAIR:(p + 1) * PAIR] for p in range(npair)]
        zs = [_dg(qh[rows], kbs[p], _NT) for p, qh in heads]
        sps = [jnp.maximum(z, 0.0) + jnp.log(1.0 + jnp.exp(-jnp.abs(z))) for z in zs]
        if diagonal:
            us = [jnp.where(causal[rows], sp, 0.0) for sp in sps]
            cs = [jnp.zeros((nrows, 1), F32)] * len(heads)
        else:
            us = sps
            cs = [c_sc[h, rows, :] for h in range(len(heads))]
        rests = [c - _mm_exact_rhs(u, later) for u, c in zip(us, cs)]
        ws = [jnp.exp((z - sp) + rest) for z, sp, rest in zip(zs, sps, rests)]
        if diagonal:
            ws = [jnp.where(causal[rows], a, 0.0) for a in ws]
        pvs = [jnp.dot(a.astype(BF16), vbs[p], preferred_element_type=F32) for (p, _), a in zip(heads, ws)]
        for p in range(npair):
            pv = jnp.where(first, pvs[2 * p], pvs[2 * p + 1])
            if diagonal:
                acc_sc[rows, p * PAIR:(p + 1) * PAIR] = pv
            else:
                acc_sc[rows, p * PAIR:(p + 1) * PAIR] += pv
        cs = [c - jnp.sum(u, axis=1, keepdims=True) for c, u in zip(cs, us)]
        for h, c in enumerate(cs):
            c_sc[h, rows, :] = c
        return cs

    def live(cs, r0, nrows):
        m = cs[0][r0:r0 + nrows]
        for c in cs[1:]:
            m = jnp.maximum(m, c[r0:r0 + nrows])
        return (jnp.max(m) > EXP_ZERO_BELOW).astype(jnp.int32)

    nsub = blk // sub
    cs = sweep(i, 0, blk, True)
    alive0 = tuple(live(cs, g * sub, sub) for g in range(nsub))

    def cond(st):
        j, alive = st
        any_alive = alive[0]
        for a in alive[1:]:
            any_alive = jnp.maximum(any_alive, a)
        return jnp.logical_and(j >= 0, any_alive > 0)

    def body(st):
        j, alive = st
        all_alive = alive[0]
        for a in alive[1:]:
            all_alive = jnp.minimum(all_alive, a)

        def whole():
            cs = sweep(j, 0, blk, False)
            return tuple(live(cs, g * sub, sub) for g in range(nsub))

        def by_group():
            return tuple(lax.cond(alive[g] > 0,
                                  lambda g=g: live(sweep(j, g * sub, sub, False), 0, sub),
                                  lambda: jnp.int32(0)) for g in range(nsub))

        return j - 1, lax.cond(all_alive > 0, whole, by_group)

    lax.while_loop(cond, body, (i - 1, alive0))
    o_ref[0] = acc_sc[...].astype(o_ref.dtype)


def _sb_attention(q, k, v):
    b, l, w = q.shape
    wb = ATT_PAIRS * PAIR
    return pl.pallas_call(
        _sb_attn_kernel,
        out_shape=jax.ShapeDtypeStruct((b, l, w), BF16),
        grid=(b, w // wb, l // ATT_BLOCK),
        in_specs=[pl.BlockSpec((1, ATT_BLOCK, wb), lambda bi, p, i: (bi, i, p)),
                  pl.BlockSpec((1, l, wb), lambda bi, p, i: (bi, 0, p)),
                  pl.BlockSpec((1, l, wb), lambda bi, p, i: (bi, 0, p))],
        out_specs=pl.BlockSpec((1, ATT_BLOCK, wb), lambda bi, p, i: (bi, i, p)),
        scratch_shapes=[pltpu.VMEM((ATT_BLOCK, wb), F32),
                        pltpu.VMEM((wb // HEAD_DIM, ATT_BLOCK, 1), F32)],
        compiler_params=_cparams(("parallel", "parallel", "arbitrary")),
        name="sb_attention",
    )(q, k, v)


def _in_proj_kernel(h_ref, w_ref, mu_ref, w0_ref, wup_ref, a0_ref, aup_ref, gup_ref,
                    kk_ref, ka_ref, bd_ref,
                    q_o, ka_o, va_o, r_o, lw_o, k_o, v_o, kn_o, b_o, g_o, prev_sc):
    t = pl.program_id(1)
    x = h_ref[0].astype(BF16)
    pall = jnp.dot(x, w_ref[...], preferred_element_type=F32)
    q_o[0] = (pall[:, :MIX_WIDTH] * (HEAD_DIM ** -0.5)).astype(BF16)
    ka_o[0] = pall[:, MIX_WIDTH:2 * MIX_WIDTH].astype(BF16)
    va_o[0] = pall[:, 2 * MIX_WIDTH:3 * MIX_WIDTH].astype(BF16)
    p = pall[:, 3 * MIX_WIDTH:]
    rows = p.shape[0]

    @pl.when(t == 0)
    def _():
        prev_sc[...] = jnp.zeros_like(prev_sc)

    row = lax.broadcasted_iota(jnp.int32, (rows, 1), 0)
    shifted = jnp.where(row == 0, prev_sc[...], pltpu.roll(p, 1, axis=0))
    prev_sc[...] = p[rows - 1:rows, :]
    xs = p + (shifted - p) * mu_ref[...]

    w = MIX_WIDTH
    r, k, v = xs[:, :w], xs[:, w:2 * w], xs[:, 2 * w:3 * w]
    wd = xs[:, 3 * w:3 * w + W_LORA]
    ad = xs[:, 3 * w + W_LORA:3 * w + W_LORA + A_LORA]
    gd = xs[:, 3 * w + W_LORA + A_LORA:]

    def lora(x, up_ref):
        return jnp.dot(x.astype(BF16), up_ref[...].astype(BF16), preferred_element_type=F32)

    lw = -DECAY_SCALE * _sigmoid(w0_ref[...] + lora(jnp.tanh(wd), wup_ref))
    a = _sigmoid(a0_ref[...] + lora(ad, aup_ref))
    g = lora(_sigmoid(gd), gup_ref)

    kk = k * kk_ref[...]
    ssq = jnp.dot((kk * kk).astype(BF16), bd_ref[...], preferred_element_type=F32)
    kn = kk / jnp.maximum(jnp.sqrt(ssq), 1e-12)
    r_o[0] = r
    lw_o[0] = lw
    k_o[0] = k * (1.0 + (a - 1.0) * ka_ref[...])
    v_o[0] = v
    kn_o[0] = kn
    b_o[0] = kn * a
    g_o[0] = g


def _in_proj(h3, w_qkv_rw, mu, w0, w_up, a0, a_up, g_up, k_k, k_a, bd512):
    b, l, d = h3.shape
    nt = l // ROW_TILE
    out = jax.ShapeDtypeStruct((b, l, MIX_WIDTH), F32)
    out16 = jax.ShapeDtypeStruct((b, l, MIX_WIDTH), BF16)
    row = pl.BlockSpec((1, ROW_TILE, MIX_WIDTH), lambda bi, t: (bi, t, 0))

    def full(arr):
        return pl.BlockSpec(arr.shape, lambda bi, t: (0,) * arr.ndim)

    consts = (w_qkv_rw, mu, w0, w_up, a0, a_up, g_up, k_k, k_a, bd512)
    return pl.pallas_call(
        _in_proj_kernel,
        out_shape=(out16,) * 3 + (out,) * 7,
        grid=(b, nt),
        in_specs=[pl.BlockSpec((1, ROW_TILE, d), lambda bi, t: (bi, t, 0))] + [full(c) for c in consts],
        out_specs=(row,) * 10,
        scratch_shapes=[pltpu.VMEM((1, RW_WIDTH), F32)],
        compiler_params=pltpu.CompilerParams(dimension_semantics=("parallel", "arbitrary"),
                                             vmem_limit_bytes=MOE_VMEM_LIMIT),
        name="in_proj",
    )(h3, *consts)


def _rwkv_scan_kernel(r_ref, lw_ref, k_ref, v_ref, kn_ref, b_ref, g_ref,
                      rk_ref, lng_ref, lnb_ref, o_ref, state_sc):
    nb, rows, w = r_ref.shape
    c = SCAN_CHUNK
    nchunk = rows // c
    npair = w // PAIR
    assert 2 * c == PAIR

    @pl.when(pl.program_id(0) == 0)
    def _():
        state_sc[...] = jnp.zeros_like(state_sc)

    rr = lax.broadcasted_iota(jnp.int32, (c, c), 0)
    cc = lax.broadcasted_iota(jnp.int32, (c, c), 1)
    ones_incl = (rr >= cc).astype(BF16)
    row_c = lax.broadcasted_iota(jnp.int32, (c, PAIR), 0)
    key_c = lax.broadcasted_iota(jnp.int32, (c, PAIR), 1) & (HEAD_DIM - 1)
    strict_m = key_c < row_c
    incl_m = key_c <= row_c
    first = lax.broadcasted_iota(jnp.int32, (1, PAIR), 1) < HEAD_DIM
    pr = lax.broadcasted_iota(jnp.int32, (PAIR, PAIR), 0)
    pc = lax.broadcasted_iota(jnp.int32, (PAIR, PAIR), 1)
    same_head = (pr < HEAD_DIM) == (pc < HEAD_DIM)
    head0_rows = lax.broadcasted_iota(jnp.int32, (PAIR, 2 * PAIR), 0) < HEAD_DIM
    head_ones = same_head.astype(BF16)
    eye_p = pr == pc
    eye_sbs = (key_c == row_c).astype(F32)
    zeros_c = jnp.zeros((c, PAIR), F32)

    def cat0(*xs):
        return jnp.concatenate(xs, axis=0)

    def cat1(*xs):
        return jnp.concatenate(xs, axis=1)

    def mm(a, b, dims=_NN):
        return _dg(a.astype(BF16), b.astype(BF16), dims)

    streams = [(bi, p, ci) for bi in range(nb) for p in range(npair) for ci in range(nchunk)]

    def sl_of(s):
        bi, p, ci = s
        return (bi, slice(ci * c, (ci + 1) * c), slice(p * PAIR, (p + 1) * PAIR))

    pre = {}
    for bi in range(nb):
        for ci in range(nchunk):
            rs = (bi, slice(ci * c, (ci + 1) * c), slice(None))
            lw = lw_ref[rs]
            cl = _mm_exact_lhs(ones_incl, lw)
            cl_end = cl[c - 1:c, :]
            e_pos = jnp.exp(cl)
            e_neg = jnp.exp(-cl)
            e_end = jnp.exp(cl_end - cl)
            r, k, kn, b = r_ref[rs], k_ref[rs], kn_ref[rs], b_ref[rs]
            full = dict(a_t=-kn * jnp.exp(cl - lw), r_t=r * e_pos, b_t=b * e_neg, k_t=k * e_neg,
                        b_h=b * e_end, k_h=k * e_end, g_end=jnp.exp(cl_end))
            for p in range(npair):
                pre[(bi, p, ci)] = {n: x[:, p * PAIR:(p + 1) * PAIR] for n, x in full.items()}

    tri = {}
    for s in streams:
        d = pre[s]
        a_t, r_t, b_t, k_t = d["a_t"], d["r_t"], d["b_t"], d["k_t"]
        bk = cat0(b_t, k_t)
        s0 = mm(jnp.where(first, cat0(a_t, r_t), 0.0), bk, _NT)
        s1t = mm(jnp.where(first, 0.0, a_t), cat0(k_t, b_t), _NT)
        s1b = mm(jnp.where(first, 0.0, r_t), bk, _NT)
        top0 = jnp.where(strict_m, s0[:c], 0.0)
        top1 = jnp.where(strict_m, s1t, 0.0)
        tri[s] = dict(
            l_ab_sbs=jnp.where(first, top0, top1),
            l_ak=cat0(jnp.where(first, 0.0, top0), jnp.where(first, top1, 0.0)),
            m0=jnp.where(incl_m, s0[c:], 0.0), m1=jnp.where(incl_m, s1b, 0.0))

    xs = {}
    for s in streams:
        v = v_ref[sl_of(s)]
        xs[s] = mm(tri[s]["l_ak"], cat0(v, v))

    def diag2(x):
        return cat0(jnp.where(first, x, 0.0), jnp.where(first, 0.0, x))

    pw = {s: tri[s]["l_ab_sbs"] for s in streams}
    tinv = {s: eye_sbs + pw[s] for s in streams}
    sq = {s: mm(pw[s], diag2(pw[s])) for s in streams}
    n = 2
    while 2 * n < c:
        for s in streams:
            both = mm(cat0(sq[s], tinv[s]), diag2(sq[s]))
            tinv[s] = tinv[s] + both[c:]
            sq[s] = both[:c]
        n *= 2
    for s in streams:
        tinv[s] = diag2(tinv[s] + mm(tinv[s], diag2(sq[s])))

    pq = {}
    for s in streams:
        a_t = pre[s]["a_t"]
        rhs = cat0(cat1(a_t, xs[s][:c]), cat1(a_t, xs[s][c:]))
        pq[s] = mm(tinv[s], rhs)

    coef = {}
    for s in streams:
        d = pre[s]
        v = v_ref[sl_of(s)]
        bkh_t = cat0(d["b_h"], d["k_h"]).T
        zv = cat1(zeros_c, v)
        res0 = mm(cat0(tri[s]["m0"], bkh_t), cat0(pq[s][:c], zv))
        res1 = mm(cat0(tri[s]["m1"], bkh_t), cat0(pq[s][c:], zv))
        ry = d["r_t"] + jnp.where(first, res0[:c, :PAIR], res1[:c, :PAIR])
        y0 = jnp.where(first, res0[:c, PAIR:], res1[:c, PAIR:])
        gj = jnp.where(head0_rows, res0[c:], res1[c:])
        gmat = jnp.where(same_head, gj[:, :PAIR], 0.0) + jnp.where(eye_p, d["g_end"], 0.0)
        jmat = jnp.where(same_head, gj[:, PAIR:], 0.0)
        coef[s] = (cat0(ry, gmat), y0, jmat)

    ys = {}
    for bi in range(nb):
        for p in range(npair):
            idx = bi * npair + p
            state = state_sc[idx]
            for ci in range(nchunk):
                lhs, y0, jmat = coef[(bi, p, ci)]
                st = mm(lhs, state)
                ys[(bi, p, ci)] = st[:c] + y0
                state = st[c:] + jmat
            state_sc[idx] = state

    for s in streams:
        sl = sl_of(s)
        csl = (slice(None), sl[2])
        y = ys[s]
        v = v_ref[sl]
        r, k = r_ref[sl], k_ref[sl]
        mean = mm(y, head_ones) * (1.0 / HEAD_DIM)
        yc = y - mean
        var = mm(yc * yc, head_ones) * (1.0 / HEAD_DIM)
        yn = yc * lax.rsqrt(var + LNX_EPS) * lng_ref[csl] + lnb_ref[csl]
        bonus = mm(r * k * rk_ref[csl], head_ones) * v
        o_ref[sl] = ((yn + bonus) * g_ref[sl]).astype(o_ref.dtype)


def _rwkv_scan(r, lw, k, v, kn, bb, g, r_k, lnx_g, lnx_b):
    b, l, w = r.shape
    c = SCAN_CHUNK * SCAN_CHUNKS_PER_STEP
    row = pl.BlockSpec((b, c, w), lambda i: (0, i, 0))
    vec = pl.BlockSpec((1, w), lambda i: (0, 0))
    return pl.pallas_call(
        _rwkv_scan_kernel,
        out_shape=jax.ShapeDtypeStruct((b, l, w), BF16),
        grid=(l // c,),
        in_specs=[row] * 7 + [vec] * 3,
        out_specs=row,
        scratch_shapes=[pltpu.VMEM((b * (w // PAIR), PAIR, PAIR), F32)],
        compiler_params=_cparams(("arbitrary",)),
        name="rwkv_scan",
    )(r, lw, k, v, kn, bb, g, r_k, lnx_g, lnx_b)


def _merge_kernel(h_ref, osb_ref, orw_ref, wg_ref, psb_ref, prw_ref, wout_ref,
                  g1_ref, b1_ref, rw_ref, rb_ref, h1_ref, comb_ref):
    h = h_ref[...]
    hb = h.astype(BF16)
    gates = _sigmoid(jnp.dot(hb, wg_ref[...], preferred_element_type=F32))
    sb = jnp.dot(osb_ref[...], psb_ref[...], preferred_element_type=F32)
    rw = jnp.dot(orw_ref[...], prw_ref[...], preferred_element_type=F32)
    merged = gates[:, :D_MODEL] * sb + gates[:, D_MODEL:] * rw
    mix = jnp.dot(merged.astype(BF16), wout_ref[...], preferred_element_type=F32)
    h1 = _layer_norm(ALPHA * h + mix, g1_ref[...], b1_ref[...])
    h1_ref[...] = h1

    logits = _mm(rw_ref[...], h1, _NT) + rb_ref[...]
    m = jnp.max(logits, axis=0, keepdims=True)
    ex = jnp.exp(logits - m)
    probs = ex / jnp.sum(ex, axis=0, keepdims=True)
    eid = lax.broadcasted_iota(jnp.int32, probs.shape, 0)
    big = N_EXPERTS

    def top2(vals):
        v1 = jnp.max(vals, axis=0, keepdims=True)
        i1 = jnp.min(jnp.where(vals == v1, eid, big), axis=0, keepdims=True)
        rest = jnp.where(eid == i1, -2.0, vals)
        v2 = jnp.max(rest, axis=0, keepdims=True)
        i2 = jnp.min(jnp.where(rest == v2, eid, big), axis=0, keepdims=True)
        return v1, i1, v2, i2

    best = None
    sel = None
    for gi in range(N_GROUPS):
        in_g = (eid >= gi * GROUP) & (eid < (gi + 1) * GROUP)
        v1, _, v2, _ = top2(jnp.where(in_g, probs, -1.0))
        score = v1 + v2
        if gi == 0:
            best, sel = score, jnp.zeros_like(score, dtype=jnp.int32)
        else:
            better = score > best
            best = jnp.where(better, score, best)
            sel = jnp.where(better, gi, sel)
    in_sel = (eid >= sel * GROUP) & (eid < (sel + 1) * GROUP)
    v1, i1, v2, i2 = top2(jnp.where(in_sel, probs, -1.0))
    denom = v1 + v2
    comb_t = jnp.where(eid == i1, v1 / denom, 0.0) + jnp.where(eid == i2, v2 / denom, 0.0)
    pad = jnp.zeros((LANE_TILE - N_EXPERTS, comb_t.shape[1]), F32)
    comb_ref[...] = jnp.concatenate([comb_t, pad], axis=0).T[:, :N_EXPERTS]


def _merge(h, o_sb, o_rw, w_gates, p_sb, p_rwkv, w_out, ln_g, ln_b, router_w, router_b):
    t, d = h.shape

    def full(arr):
        return pl.BlockSpec(arr.shape, lambda i: (0,) * arr.ndim)

    consts = (w_gates, p_sb, p_rwkv, w_out, ln_g, ln_b, router_w, router_b)
    return pl.pallas_call(
        _merge_kernel,
        out_shape=(jax.ShapeDtypeStruct((t, d), F32), jax.ShapeDtypeStruct((t, N_EXPERTS), F32)),
        grid=(t // ROW_TILE,),
        in_specs=[pl.BlockSpec((ROW_TILE, d), lambda i: (i, 0)),
                  pl.BlockSpec((ROW_TILE, MIX_WIDTH), lambda i: (i, 0)),
                  pl.BlockSpec((ROW_TILE, MIX_WIDTH), lambda i: (i, 0))] + [full(c) for c in consts],
        out_specs=(pl.BlockSpec((ROW_TILE, d), lambda i: (i, 0)),
                   pl.BlockSpec((ROW_TILE, N_EXPERTS), lambda i: (i, 0))),
        compiler_params=_cparams(("parallel",)),
        name="merge_ln1_router",
    )(h, o_sb, o_rw, *consts)


def _moe_kernel(h_ref, comb_ref, wg_ref, wu_ref, wd_ref, g2_ref, b2_ref, o_ref,
                xs_sc, ys_sc, col_sc, seg_sm):
    e = pl.program_id(1)
    tm, d = h_ref.shape
    ne = comb_ref.shape[1]
    blk = MOE_SORT_BLOCK
    chunk = MOE_CHUNK
    ns = xs_sc.shape[0] - chunk

    @pl.when(e == 0)
    def _dispatch():
        comb = comb_ref[...]
        sel = comb != 0.0
        self32 = jnp.where(sel, 1.0, 0.0)
        selb = self32.astype(BF16)
        lane = lax.broadcasted_iota(jnp.int32, (tm, ne), 1)
        rr = lax.broadcasted_iota(jnp.int32, (blk, blk), 0)
        cc = lax.broadcasted_iota(jnp.int32, (blk, blk), 1)
        before = jnp.where(cc < rr, 1.0, 0.0).astype(BF16)
        ranks = []
        seen = jnp.zeros((1, ne), F32)
        for r0 in range(0, tm, blk):
            ranks.append(jnp.dot(before, selb[r0:r0 + blk], preferred_element_type=F32) + seen)
            seen = seen + jnp.sum(self32[r0:r0 + blk], axis=0, keepdims=True)
        rank = jnp.concatenate(ranks, axis=0)
        counts = jnp.sum(self32, axis=0, keepdims=True).astype(jnp.int32)
        lane1 = lax.broadcasted_iota(jnp.int32, (1, ne), 1)
        off = jnp.int32(0)
        offv = jnp.zeros((1, ne), F32)
        for ex in range(ne):
            n = counts[0, ex]
            seg_sm[0, ex] = off
            seg_sm[1, ex] = n
            offv = jnp.where(lane1 == ex, off.astype(F32), offv)
            off = off + ((n + (MOE_ALIGN - 1)) // MOE_ALIGN) * MOE_ALIGN
        pos = offv + rank

        def pick(idx, arr):
            return jnp.sum(jnp.where(lane == idx, arr, 0.0), axis=1, keepdims=True)

        i_lo = jnp.min(jnp.where(sel, lane, ne), axis=1, keepdims=True)
        i_hi = jnp.max(jnp.where(sel, lane, -1), axis=1, keepdims=True)
        any_sel = i_hi >= 0
        two = i_hi != i_lo
        p_lo = jnp.where(any_sel, pick(i_lo, pos), -1.0)
        p_hi = jnp.where(two, pick(i_hi, pos), -1.0)
        g_lo = pick(i_lo, comb)
        g_hi = jnp.where(two, pick(i_hi, comb), 0.0)
        l128 = lax.broadcasted_iota(jnp.int32, (tm, 128), 1)
        cols = jnp.where(l128 == 0, p_lo, jnp.where(l128 == 1, p_hi, jnp.where(l128 == 2, g_lo,
                         jnp.where(l128 == 3, g_hi, 0.0))))
        col_sc[...] = cols
        rows_t = cols.T
        p_lo_r, p_hi_r = rows_t[0:1, :], rows_t[1:2, :]
        x = h_ref[...].astype(BF16)
        for r0 in range(0, ns, blk):
            ri = (lax.broadcasted_iota(jnp.int32, (blk, tm), 0) + r0).astype(F32)
            onehot = jnp.where((ri == p_lo_r) | (ri == p_hi_r), 1.0, 0.0).astype(BF16)
            xs_sc[r0:r0 + blk, :] = jnp.dot(onehot, x, preferred_element_type=F32).astype(BF16)
        xs_sc[ns:, :] = jnp.zeros((chunk, d), BF16)
        ys_sc[...] = jnp.zeros_like(ys_sc)

    for sub in range(wg_ref.shape[1]):
        ex = e * wg_ref.shape[1] + sub
        seg_off = seg_sm[0, ex]
        seg_n = seg_sm[1, ex]

        def ffn_chunk(c, carry, sub=sub, seg_off=seg_off):
            start = pl.multiple_of(seg_off + c * chunk, MOE_ALIGN)
            x = xs_sc[pl.ds(start, chunk), :]
            gate = jnp.dot(x, wg_ref[0, sub], preferred_element_type=F32)
            up = jnp.dot(x, wu_ref[0, sub], preferred_element_type=F32)
            hid = gate * _sigmoid(gate) * up
            ys_sc[pl.ds(start, chunk), :] = jnp.dot(hid.astype(BF16), wd_ref[0, sub],
                                                    preferred_element_type=F32).astype(BF16)
            return carry

        lax.fori_loop(0, (seg_n + (chunk - 1)) // chunk, ffn_chunk, 0)

    @pl.when(e == pl.num_programs(1) - 1)
    def _combine():
        ys = ys_sc[0:ns, :]
        for r0 in range(0, tm, blk):
            cols = col_sc[r0:r0 + blk, :]
            p_lo, p_hi, g_lo, g_hi = cols[:, 0:1], cols[:, 1:2], cols[:, 2:3], cols[:, 3:4]
            li = lax.broadcasted_iota(jnp.int32, (blk, ns), 1).astype(F32)
            wsel = jnp.where(li == p_lo, g_lo, 0.0) + jnp.where(li == p_hi, g_hi, 0.0)
            ff = jnp.dot(wsel.astype(BF16), ys, preferred_element_type=F32)
            o_ref[r0:r0 + blk, :] = _layer_norm(ALPHA * h_ref[r0:r0 + blk, :] + ff, g2_ref[...], b2_ref[...])


def _moe(h, comb, wg, wu, wd, li, ln_g, ln_b):
    t, d = h.shape
    _, ne, _, f = wg.shape
    rows = MOE_ROW_TILE
    eps = MOE_EXPERTS_PER_STEP
    ns = -(-(2 * rows + ne * (MOE_ALIGN - 1)) // MOE_SORT_BLOCK) * MOE_SORT_BLOCK
    return pl.pallas_call(
        _moe_kernel,
        out_shape=jax.ShapeDtypeStruct((t, d), F32),
        grid=(t // rows, ne // eps),
        in_specs=[pl.BlockSpec((rows, d), lambda i, e: (i, 0), pipeline_mode=pl.Buffered(1)),
                  pl.BlockSpec((rows, ne), lambda i, e: (i, 0)),
                  pl.BlockSpec((1, eps, d, f), lambda i, e: (li, e, 0, 0)),
                  pl.BlockSpec((1, eps, d, f), lambda i, e: (li, e, 0, 0)),
                  pl.BlockSpec((1, eps, f, d), lambda i, e: (li, e, 0, 0)),
                  pl.BlockSpec((1, d), lambda i, e: (0, 0)),
                  pl.BlockSpec((1, d), lambda i, e: (0, 0))],
        out_specs=pl.BlockSpec((rows, d), lambda i, e: (i, 0)),
        scratch_shapes=[pltpu.VMEM((ns + MOE_CHUNK, d), BF16),
                        pltpu.VMEM((ns + MOE_CHUNK, d), BF16),
                        pltpu.VMEM((rows, 128), F32),
                        pltpu.SMEM((2, ne), jnp.int32)],
        compiler_params=pltpu.CompilerParams(dimension_semantics=("parallel", "arbitrary"),
                                             vmem_limit_bytes=MOE_VMEM_LIMIT),
        name="moe_ln2",
    )(h, comb, wg, wu, wd, ln_g, ln_b)


def _layer(h, lp, shared, batch):
    t, d = h.shape
    l = t // batch
    proj = _in_proj(h.reshape(batch, l, d), lp["w_qkv_rw"], lp["mu"], lp["w0"], lp["w_up"], lp["a0"],
                    lp["a_up"], lp["g_up"], lp["k_k"], lp["k_a"], shared["bd512"])
    o_sb = _sb_attention(*proj[:3]).reshape(t, MIX_WIDTH)
    o_rw = _rwkv_scan(*proj[3:], lp["r_k"], lp["lnx_g"], lp["lnx_b"]).reshape(t, MIX_WIDTH)
    h1, comb = _merge(h, o_sb, o_rw, lp["w_gates"], lp["p_sb"], lp["p_rwkv"], lp["w_out"],
                      lp["ln1_g"], lp["ln1_b"], shared["router_w"], shared["router_b"])
    return _moe(h1, comb, shared["wg"], shared["wu"], shared["wd"], lp["li"], lp["ln2_g"], lp["ln2_b"])


def kernel(x, meta, emb_ln_g, emb_ln_b, w_in, rwkv_mu, w0, w_up, a0, a_up, g_up, k_k, k_a, r_k, lnx_g, lnx_b, p_sb, p_rwkv, w_out, ln1_g, ln1_b, router_w, router_b, exp_w_gate, exp_w_up, exp_w_down, ln2_g, ln2_b):
    batch, seq, d = x.shape
    n_meta = meta.shape[0]
    l = seq + n_meta
    lpad = -(-l // SEQ_ALIGN) * SEQ_ALIGN
    depth = w_in.shape[0]

    h = _embed_ln(x, meta.astype(x.dtype), emb_ln_g.reshape(1, d), emb_ln_b.reshape(1, d), lpad)
    h = h.reshape(batch * lpad, d)

    idx = jnp.arange(MIX_WIDTH) // HEAD_DIM
    shared = {
        "bd512": (idx[:, None] == idx[None, :]).astype(BF16),
        "router_w": router_w.T,
        "wg": exp_w_gate.astype(BF16), "wu": exp_w_up.astype(BF16), "wd": exp_w_down.astype(BF16),
        "router_b": router_b.reshape(-1, 1),
    }
    qkv_w = 3 * MIX_WIDTH
    for li in range(depth):
        row = lambda a: a[li].reshape(1, -1)
        lp = {
            "w_qkv_rw": w_in[li, :, :qkv_w + RW_WIDTH].astype(BF16),
            "w_gates": w_in[li, :, qkv_w + RW_WIDTH:].astype(BF16),
            "mu": row(rwkv_mu), "w0": row(w0), "a0": row(a0), "k_k": row(k_k), "k_a": row(k_a),
            "w_up": w_up[li], "a_up": a_up[li], "g_up": g_up[li],
            "r_k": row(r_k), "lnx_g": row(lnx_g), "lnx_b": row(lnx_b),
            "p_sb": p_sb[li].astype(BF16), "p_rwkv": p_rwkv[li].astype(BF16),
            "w_out": w_out[li].astype(BF16),
            "ln1_g": row(ln1_g), "ln1_b": row(ln1_b), "ln2_g": row(ln2_g), "ln2_b": row(ln2_b),
            "li": li,
        }
        h = _layer(h, lp, shared, batch)
    return h.reshape(batch, lpad, d)[:, n_meta:l]
```

```python
import functools
import math

import jax
import jax.numpy as jnp
from jax import lax
from jax.experimental import pallas as pl
from jax.experimental.pallas import tpu as pltpu

F32 = jnp.float32
BF16 = jnp.bfloat16

D_MODEL = 1024
DEPTH = 2
N_META = 16
HEAD_DIM = 64
PAIR = 2 * HEAD_DIM
LANE_TILE = 128
MIX_WIDTH = 512
W_LORA, A_LORA, G_LORA = 64, 64, 128
RW_WIDTH = 3 * MIX_WIDTH + W_LORA + A_LORA + G_LORA
DECAY_SCALE = math.exp(-0.5)
LNX_EPS = 64e-5
LN_EPS = 1e-5
N_EXPERTS = 16
N_GROUPS = 4
GROUP = N_EXPERTS // N_GROUPS
D_EXPERT = 512
ALPHA = (2 * DEPTH) ** 0.25

SEQ_ALIGN = 640
ROW_TILE = 640
ATT_BLOCK = 128
ATT_SUB = 32
ATT_PAIRS = 4
SCAN_CHUNK = 64
SCAN_CHUNKS_PER_STEP = 2
EXP_ZERO_BELOW = -104.0
MOE_ROW_TILE = 1280
MOE_CHUNK = 192
MOE_SORT_BLOCK = 256
MOE_EXPERTS_PER_STEP = 2
MOE_ALIGN = 16
VMEM_LIMIT = 48 * 1024 * 1024
MOE_VMEM_LIMIT = 58 * 1024 * 1024


def _cparams(sem):
    return pltpu.CompilerParams(dimension_semantics=sem, vmem_limit_bytes=VMEM_LIMIT)


def _split2(x):
    hi = x.astype(BF16)
    lo = (x - hi.astype(F32)).astype(BF16)
    return hi, lo


_NN = (((1,), (0,)), ((), ()))
_NT = (((1,), (1,)), ((), ()))


def _dg(a, b, dims):
    return lax.dot_general(a, b, dims, preferred_element_type=F32)


def _mm(a, b, dims=_NN):
    ah, al = _split2(a)
    bh, bl = _split2(b)
    return _dg(ah, bh, dims) + (_dg(ah, bl, dims) + _dg(al, bh, dims))


def _mm_exact_lhs(a_bf16, b, dims=_NN):
    b1, b2 = _split2(b)
    return _dg(a_bf16, b1, dims) + _dg(a_bf16, b2, dims)


def _mm_exact_rhs(a, b_bf16, dims=_NN):
    a1, a2 = _split2(a)
    return _dg(a1, b_bf16, dims) + _dg(a2, b_bf16, dims)


def _sigmoid(x):
    return 1.0 / (1.0 + jnp.exp(-x))


def _layer_norm(x, g, b):
    mu = jnp.mean(x, axis=-1, keepdims=True)
    xc = x - mu
    var = jnp.mean(xc * xc, axis=-1, keepdims=True)
    return xc * lax.rsqrt(var + LN_EPS) * g + b


def _embed_ln_kernel(x_ref, tail_ref, meta_ref, g_ref, b_ref, o_ref, *, n_real):
    i = pl.program_id(1)
    rows, n_meta = o_ref.shape[1], meta_ref.shape[0]
    head = jnp.where(i == 0, meta_ref[...], tail_ref[0])
    tile = jnp.concatenate([head, x_ref[0, :rows - n_meta, :]], axis=0)
    pos = i * rows + lax.broadcasted_iota(jnp.int32, (rows, 1), 0)
    tile = jnp.where(pos < n_real, tile, 0.0)
    o_ref[0] = _layer_norm(tile, g_ref[...], b_ref[...])


def _embed_ln(x, meta, g, b, lpad):
    bsz, seq, d = x.shape
    n_meta = meta.shape[0]
    rows = ROW_TILE
    per = rows // n_meta
    return pl.pallas_call(
        functools.partial(_embed_ln_kernel, n_real=seq + n_meta),
        out_shape=jax.ShapeDtypeStruct((bsz, lpad, d), F32),
        grid=(bsz, lpad // rows),
        in_specs=[pl.BlockSpec((1, rows, d), lambda bi, i: (bi, i, 0)),
                  pl.BlockSpec((1, n_meta, d), lambda bi, i: (bi, jnp.maximum(i * per - 1, 0), 0)),
                  pl.BlockSpec((n_meta, d), lambda bi, i: (0, 0)),
                  pl.BlockSpec((1, d), lambda bi, i: (0, 0)),
                  pl.BlockSpec((1, d), lambda bi, i: (0, 0))],
        out_specs=pl.BlockSpec((1, rows, d), lambda bi, i: (bi, i, 0)),
        compiler_params=_cparams(("parallel", "parallel")),
        name="embed_ln",
    )(x, x, meta, g, b)


def _sb_attn_kernel(q_ref, k_ref, v_ref, o_ref, acc_sc, c_sc):
    blk = ATT_BLOCK
    sub = ATT_SUB
    npair = q_ref.shape[2] // PAIR
    i = pl.program_id(2)
    first = lax.broadcasted_iota(jnp.int32, (1, PAIR), 1) < HEAD_DIM
    heads = []
    for p in range(npair):
        q = q_ref[0, :, p * PAIR:(p + 1) * PAIR]
        zero = jnp.zeros_like(q)
        heads += [(p, jnp.where(first, q, zero)), (p, jnp.where(first, zero, q))]
    rr = lax.broadcasted_iota(jnp.int32, (blk, blk), 0)
    cc = lax.broadcasted_iota(jnp.int32, (blk, blk), 1)
    causal = cc < rr
    later = (rr > cc).astype(BF16)

    def sweep(j, r0, nrows, diagonal):
        rows = slice(r0, r0 + nrows)
        start = pl.multiple_of(j * blk, blk)
        kbs = [k_ref[0, pl.ds(start, blk), p * PAIR:(p + 1) * PAIR] for p in range(npair)]
        vbs = [v_ref[0, pl.ds(start, blk), p * PAIR:(p + 1) * PAIR] for p in range(npair)]
        zs = [_dg(qh[rows], kbs[p], _NT) for p, qh in heads]
        sps = [jnp.maximum(z, 0.0) + jnp.log(1.0 + jnp.exp(-jnp.abs(z))) for z in zs]
        if diagonal:
            us = [jnp.where(causal[rows], sp, 0.0) for sp in sps]
            cs = [jnp.zeros((nrows, 1), F32)] * len(heads)
        else:
            us = sps
            cs = [c_sc[h, rows, :] for h in range(len(heads))]
        rests = [c - _mm_exact_rhs(u, later) for u, c in zip(us, cs)]
        ws = [jnp.exp((z - sp) + rest) for z, sp, rest in zip(zs, sps, rests)]
        if diagonal:
            ws = [jnp.where(causal[rows], a, 0.0) for a in ws]
        pvs = [jnp.dot(a.astype(BF16), vbs[p], preferred_element_type=F32) for (p, _), a in zip(heads, ws)]
        for p in range(npair):
            pv = jnp.where(first, pvs[2 * p], pvs[2 * p + 1])
            if diagonal:
                acc_sc[rows, p * PAIR:(p + 1) * PAIR] = pv
            else:
                acc_sc[rows, p * PAIR:(p + 1) * PAIR] += pv
        cs = [c - jnp.sum(u, axis=1, keepdims=True) for c, u in zip(cs, us)]
        for h, c in enumerate(cs):
            c_sc[h, rows, :] = c
        return cs

    def live(cs, r0, nrows):
        m = cs[0][r0:r0 + nrows]
        for c in cs[1:]:
            m = jnp.maximum(m, c[r0:r0 + nrows])
        return (jnp.max(m) > EXP_ZERO_BELOW).astype(jnp.int32)

    nsub = blk // sub
    cs = sweep(i, 0, blk, True)
    alive0 = tuple(live(cs, g * sub, sub) for g in range(nsub))

    def cond(st):
        j, alive = st
        any_alive = alive[0]
        for a in alive[1:]:
            any_alive = jnp.maximum(any_alive, a)
        return jnp.logical_and(j >= 0, any_alive > 0)

    def body(st):
        j, alive = st
        all_alive = alive[0]
        for a in alive[1:]:
            all_alive = jnp.minimum(all_alive, a)

        def whole():
            cs = sweep(j, 0, blk, False)
            return tuple(live(cs, g * sub, sub) for g in range(nsub))

        def by_group():
            return tuple(lax.cond(alive[g] > 0,
                                  lambda g=g: live(sweep(j, g * sub, sub, False), 0, sub),
                                  lambda: jnp.int32(0)) for g in range(nsub))

        return j - 1, lax.cond(all_alive > 0, whole, by_group)

    lax.while_loop(cond, body, (i - 1, alive0))
    o_ref[0] = acc_sc[...].astype(o_ref.dtype)


def _sb_attention(q, k, v):
    b, l, w = q.shape
    wb = ATT_PAIRS * PAIR
    return pl.pallas_call(
        _sb_attn_kernel,
        out_shape=jax.ShapeDtypeStruct((b, l, w), BF16),
        grid=(b, w // wb, l // ATT_BLOCK),
        in_specs=[pl.BlockSpec((1, ATT_BLOCK, wb), lambda bi, p, i: (bi, i, p)),
                  pl.BlockSpec((1, l, wb), lambda bi, p, i: (bi, 0, p)),
                  pl.BlockSpec((1, l, wb), lambda bi, p, i: (bi, 0, p))],
        out_specs=pl.BlockSpec((1, ATT_BLOCK, wb), lambda bi, p, i: (bi, i, p)),
        scratch_shapes=[pltpu.VMEM((ATT_BLOCK, wb), F32),
                        pltpu.VMEM((wb // HEAD_DIM, ATT_BLOCK, 1), F32)],
        compiler_params=_cparams(("parallel", "parallel", "arbitrary")),
        name="sb_attention",
    )(q, k, v)


def _in_proj_kernel(h_ref, w_ref, mu_ref, w0_ref, wup_ref, a0_ref, aup_ref, gup_ref,
                    kk_ref, ka_ref, bd_ref,
                    q_o, ka_o, va_o, r_o, lw_o, k_o, v_o, kn_o, b_o, g_o, prev_sc):
    t = pl.program_id(1)
    x = h_ref[0].astype(BF16)
    pall = jnp.dot(x, w_ref[...], preferred_element_type=F32)
    q_o[0] = (pall[:, :MIX_WIDTH] * (HEAD_DIM ** -0.5)).astype(BF16)
    ka_o[0] = pall[:, MIX_WIDTH:2 * MIX_WIDTH].astype(BF16)
    va_o[0] = pall[:, 2 * MIX_WIDTH:3 * MIX_WIDTH].astype(BF16)
    p = pall[:, 3 * MIX_WIDTH:]
    rows = p.shape[0]

    @pl.when(t == 0)
    def _():
        prev_sc[...] = jnp.zeros_like(prev_sc)

    row = lax.broadcasted_iota(jnp.int32, (rows, 1), 0)
    shifted = jnp.where(row == 0, prev_sc[...], pltpu.roll(p, 1, axis=0))
    prev_sc[...] = p[rows - 1:rows, :]
    xs = p + (shifted - p) * mu_ref[...]

    w = MIX_WIDTH
    r, k, v = xs[:, :w], xs[:, w:2 * w], xs[:, 2 * w:3 * w]
    wd = xs[:, 3 * w:3 * w + W_LORA]
    ad = xs[:, 3 * w + W_LORA:3 * w + W_LORA + A_LORA]
    gd = xs[:, 3 * w + W_LORA + A_LORA:]

    def lora(x, up_ref):
        return jnp.dot(x.astype(BF16), up_ref[...].astype(BF16), preferred_element_type=F32)

    lw = -DECAY_SCALE * _sigmoid(w0_ref[...] + lora(jnp.tanh(wd), wup_ref))
    a = _sigmoid(a0_ref[...] + lora(ad, aup_ref))
    g = lora(_sigmoid(gd), gup_ref)

    kk = k * kk_ref[...]
    ssq = jnp.dot((kk * kk).astype(BF16), bd_ref[...], preferred_element_type=F32)
    kn = kk / jnp.maximum(jnp.sqrt(ssq), 1e-12)
    r_o[0] = r
    lw_o[0] = lw
    k_o[0] = k * (1.0 + (a - 1.0) * ka_ref[...])
    v_o[0] = v
    kn_o[0] = kn
    b_o[0] = kn * a
    g_o[0] = g


def _in_proj(h3, w_qkv_rw, mu, w0, w_up, a0, a_up, g_up, k_k, k_a, bd512):
    b, l, d = h3.shape
    nt = l // ROW_TILE
    out = jax.ShapeDtypeStruct((b, l, MIX_WIDTH), F32)
    out16 = jax.ShapeDtypeStruct((b, l, MIX_WIDTH), BF16)
    row = pl.BlockSpec((1, ROW_TILE, MIX_WIDTH), lambda bi, t: (bi, t, 0))

    def full(arr):
        return pl.BlockSpec(arr.shape, lambda bi, t: (0,) * arr.ndim)

    consts = (w_qkv_rw, mu, w0, w_up, a0, a_up, g_up, k_k, k_a, bd512)
    return pl.pallas_call(
        _in_proj_kernel,
        out_shape=(out16,) * 3 + (out,) * 7,
        grid=(b, nt),
        in_specs=[pl.BlockSpec((1, ROW_TILE, d), lambda bi, t: (bi, t, 0))] + [full(c) for c in consts],
        out_specs=(row,) * 10,
        scratch_shapes=[pltpu.VMEM((1, RW_WIDTH), F32)],
        compiler_params=pltpu.CompilerParams(dimension_semantics=("parallel", "arbitrary"),
                                             vmem_limit_bytes=MOE_VMEM_LIMIT),
        name="in_proj",
    )(h3, *consts)


def _rwkv_scan_kernel(r_ref, lw_ref, k_ref, v_ref, kn_ref, b_ref, g_ref,
                      rk_ref, lng_ref, lnb_ref, o_ref, state_sc):
    nb, rows, w = r_ref.shape
    c = SCAN_CHUNK
    nchunk = rows // c
    npair = w // PAIR
    assert 2 * c == PAIR

    @pl.when(pl.program_id(0) == 0)
    def _():
        state_sc[...] = jnp.zeros_like(state_sc)

    rr = lax.broadcasted_iota(jnp.int32, (c, c), 0)
    cc = lax.broadcasted_iota(jnp.int32, (c, c), 1)
    ones_incl = (rr >= cc).astype(BF16)
    row_c = lax.broadcasted_iota(jnp.int32, (c, PAIR), 0)
    key_c = lax.broadcasted_iota(jnp.int32, (c, PAIR), 1) & (HEAD_DIM - 1)
    strict_m = key_c < row_c
    incl_m = key_c <= row_c
    first = lax.broadcasted_iota(jnp.int32, (1, PAIR), 1) < HEAD_DIM
    pr = lax.broadcasted_iota(jnp.int32, (PAIR, PAIR), 0)
    pc = lax.broadcasted_iota(jnp.int32, (PAIR, PAIR), 1)
    same_head = (pr < HEAD_DIM) == (pc < HEAD_DIM)
    head0_rows = lax.broadcasted_iota(jnp.int32, (PAIR, 2 * PAIR), 0) < HEAD_DIM
    head_ones = same_head.astype(BF16)
    eye_p = pr == pc
    eye_sbs = (key_c == row_c).astype(F32)
    zeros_c = jnp.zeros((c, PAIR), F32)

    def cat0(*xs):
        return jnp.concatenate(xs, axis=0)

    def cat1(*xs):
        return jnp.concatenate(xs, axis=1)

    def mm(a, b, dims=_NN):
        return _dg(a.astype(BF16), b.astype(BF16), dims)

    streams = [(bi, p, ci) for bi in range(nb) for p in range(npair) for ci in range(nchunk)]

    def sl_of(s):
        bi, p, ci = s
        return (bi, slice(ci * c, (ci + 1) * c), slice(p * PAIR, (p + 1) * PAIR))

    pre = {}
    for bi in range(nb):
        for ci in range(nchunk):
            rs = (bi, slice(ci * c, (ci + 1) * c), slice(None))
            lw = lw_ref[rs]
            cl = _mm_exact_lhs(ones_incl, lw)
            cl_end = cl[c - 1:c, :]
            e_pos = jnp.exp(cl)
            e_neg = jnp.exp(-cl)
            e_end = jnp.exp(cl_end - cl)
            r, k, kn, b = r_ref[rs], k_ref[rs], kn_ref[rs], b_ref[rs]
            full = dict(a_t=-kn * jnp.exp(cl - lw), r_t=r * e_pos, b_t=b * e_neg, k_t=k * e_neg,
                        b_h=b * e_end, k_h=k * e_end, g_end=jnp.exp(cl_end))
            for p in range(npair):
                pre[(bi, p, ci)] = {n: x[:, p * PAIR:(p + 1) * PAIR] for n, x in full.items()}

    tri = {}
    for s in streams:
        d = pre[s]
        a_t, r_t, b_t, k_t = d["a_t"], d["r_t"], d["b_t"], d["k_t"]
        bk = cat0(b_t, k_t)
        s0 = mm(jnp.where(first, cat0(a_t, r_t), 0.0), bk, _NT)
        s1t = mm(jnp.where(first, 0.0, a_t), cat0(k_t, b_t), _NT)
        s1b = mm(jnp.where(first, 0.0, r_t), bk, _NT)
        top0 = jnp.where(strict_m, s0[:c], 0.0)
        top1 = jnp.where(strict_m, s1t, 0.0)
        tri[s] = dict(
            l_ab_sbs=jnp.where(first, top0, top1),
            l_ak=cat0(jnp.where(first, 0.0, top0), jnp.where(first, top1, 0.0)),
            m0=jnp.where(incl_m, s0[c:], 0.0), m1=jnp.where(incl_m, s1b, 0.0))

    xs = {}
    for s in streams:
        v = v_ref[sl_of(s)]
        xs[s] = mm(tri[s]["l_ak"], cat0(v, v))

    def diag2(x):
        return cat0(jnp.where(first, x, 0.0), jnp.where(first, 0.0, x))

    pw = {s: tri[s]["l_ab_sbs"] for s in streams}
    tinv = {s: eye_sbs + pw[s] for s in streams}
    sq = {s: mm(pw[s], diag2(pw[s])) for s in streams}
    n = 2
    while 2 * n < c:
        for s in streams:
            both = mm(cat0(sq[s], tinv[s]), diag2(sq[s]))
            tinv[s] = tinv[s] + both[c:]
            sq[s] = both[:c]
        n *= 2
    for s in streams:
        tinv[s] = diag2(tinv[s] + mm(tinv[s], diag2(sq[s])))

    pq = {}
    for s in streams:
        a_t = pre[s]["a_t"]
        rhs = cat0(cat1(a_t, xs[s][:c]), cat1(a_t, xs[s][c:]))
        pq[s] = mm(tinv[s], rhs)

    coef = {}
    for s in streams:
        d = pre[s]
        v = v_ref[sl_of(s)]
        bkh_t = cat0(d["b_h"], d["k_h"]).T
        zv = cat1(zeros_c, v)
        res0 = mm(cat0(tri[s]["m0"], bkh_t), cat0(pq[s][:c], zv))
        res1 = mm(cat0(tri[s]["m1"], bkh_t), cat0(pq[s][c:], zv))
        ry = d["r_t"] + jnp.where(first, res0[:c, :PAIR], res1[:c, :PAIR])
        y0 = jnp.where(first, res0[:c, PAIR:], res1[:c, PAIR:])
        gj = jnp.where(head0_rows, res0[c:], res1[c:])
        gmat = jnp.where(same_head, gj[:, :PAIR], 0.0) + jnp.where(eye_p, d["g_end"], 0.0)
        jmat = jnp.where(same_head, gj[:, PAIR:], 0.0)
        coef[s] = (cat0(ry, gmat), y0, jmat)

    ys = {}
    for bi in range(nb):
        for p in range(npair):
            idx = bi * npair + p
            state = state_sc[idx]
            for ci in range(nchunk):
                lhs, y0, jmat = coef[(bi, p, ci)]
                st = mm(lhs, state)
                ys[(bi, p, ci)] = st[:c] + y0
                state = st[c:] + jmat
            state_sc[idx] = state

    for s in streams:
        sl = sl_of(s)
        csl = (slice(None), sl[2])
        y = ys[s]
        v = v_ref[sl]
        r, k = r_ref[sl], k_ref[sl]
        mean = mm(y, head_ones) * (1.0 / HEAD_DIM)
        yc = y - mean
        var = mm(yc * yc, head_ones) * (1.0 / HEAD_DIM)
        yn = yc * lax.rsqrt(var + LNX_EPS) * lng_ref[csl] + lnb_ref[csl]
        bonus = mm(r * k * rk_ref[csl], head_ones) * v
        o_ref[sl] = ((yn + bonus) * g_ref[sl]).astype(o_ref.dtype)


def _rwkv_scan(r, lw, k, v, kn, bb, g, r_k, lnx_g, lnx_b):
    b, l, w = r.shape
    c = SCAN_CHUNK * SCAN_CHUNKS_PER_STEP
    row = pl.BlockSpec((b, c, w), lambda i: (0, i, 0))
    vec = pl.BlockSpec((1, w), lambda i: (0, 0))
    return pl.pallas_call(
        _rwkv_scan_kernel,
        out_shape=jax.ShapeDtypeStruct((b, l, w), BF16),
        grid=(l // c,),
        in_specs=[row] * 7 + [vec] * 3,
        out_specs=row,
        scratch_shapes=[pltpu.VMEM((b * (w // PAIR), PAIR, PAIR), F32)],
        compiler_params=_cparams(("arbitrary",)),
        name="rwkv_scan",
    )(r, lw, k, v, kn, bb, g, r_k, lnx_g, lnx_b)


def _merge_kernel(h_ref, osb_ref, orw_ref, wg_ref, psb_ref, prw_ref, wout_ref,
                  g1_ref, b1_ref, rw_ref, rb_ref, h1_ref, comb_ref):
    h = h_ref[...]
    hb = h.astype(BF16)
    gates = _sigmoid(jnp.dot(hb, wg_ref[...], preferred_element_type=F32))
    sb = jnp.dot(osb_ref[...], psb_ref[...], preferred_element_type=F32)
    rw = jnp.dot(orw_ref[...], prw_ref[...], preferred_element_type=F32)
    merged = gates[:, :D_MODEL] * sb + gates[:, D_MODEL:] * rw
    mix = jnp.dot(merged.astype(BF16), wout_ref[...], preferred_element_type=F32)
    h1 = _layer_norm(ALPHA * h + mix, g1_ref[...], b1_ref[...])
    h1_ref[...] = h1

    logits = _mm(rw_ref[...], h1, _NT) + rb_ref[...]
    m = jnp.max(logits, axis=0, keepdims=True)
    ex = jnp.exp(logits - m)
    probs = ex / jnp.sum(ex, axis=0, keepdims=True)
    eid = lax.broadcasted_iota(jnp.int32, probs.shape, 0)
    big = N_EXPERTS

    def top2(vals):
        v1 = jnp.max(vals, axis=0, keepdims=True)
        i1 = jnp.min(jnp.where(vals == v1, eid, big), axis=0, keepdims=True)
        rest = jnp.where(eid == i1, -2.0, vals)
        v2 = jnp.max(rest, axis=0, keepdims=True)
        i2 = jnp.min(jnp.where(rest == v2, eid, big), axis=0, keepdims=True)
        return v1, i1, v2, i2

    best = None
    sel = None
    for gi in range(N_GROUPS):
        in_g = (eid >= gi * GROUP) & (eid < (gi + 1) * GROUP)
        v1, _, v2, _ = top2(jnp.where(in_g, probs, -1.0))
        score = v1 + v2
        if gi == 0:
            best, sel = score, jnp.zeros_like(score, dtype=jnp.int32)
        else:
            better = score > best
            best = jnp.where(better, score, best)
            sel = jnp.where(better, gi, sel)
    in_sel = (eid >= sel * GROUP) & (eid < (sel + 1) * GROUP)
    v1, i1, v2, i2 = top2(jnp.where(in_sel, probs, -1.0))
    denom = v1 + v2
    comb_t = jnp.where(eid == i1, v1 / denom, 0.0) + jnp.where(eid == i2, v2 / denom, 0.0)
    pad = jnp.zeros((LANE_TILE - N_EXPERTS, comb_t.shape[1]), F32)
    comb_ref[...] = jnp.concatenate([comb_t, pad], axis=0).T[:, :N_EXPERTS]


def _merge(h, o_sb, o_rw, w_gates, p_sb, p_rwkv, w_out, ln_g, ln_b, router_w, router_b):
    t, d = h.shape

    def full(arr):
        return pl.BlockSpec(arr.shape, lambda i: (0,) * arr.ndim)

    consts = (w_gates, p_sb, p_rwkv, w_out, ln_g, ln_b, router_w, router_b)
    return pl.pallas_call(
        _merge_kernel,
        out_shape=(jax.ShapeDtypeStruct((t, d), F32), jax.ShapeDtypeStruct((t, N_EXPERTS), F32)),
        grid=(t // ROW_TILE,),
        in_specs=[pl.BlockSpec((ROW_TILE, d), lambda i: (i, 0)),
                  pl.BlockSpec((ROW_TILE, MIX_WIDTH), lambda i: (i, 0)),
                  pl.BlockSpec((ROW_TILE, MIX_WIDTH), lambda i: (i, 0))] + [full(c) for c in consts],
        out_specs=(pl.BlockSpec((ROW_TILE, d), lambda i: (i, 0)),
                   pl.BlockSpec((ROW_TILE, N_EXPERTS), lambda i: (i, 0))),
        compiler_params=_cparams(("parallel",)),
        name="merge_ln1_router",
    )(h, o_sb, o_rw, *consts)


def _moe_kernel(h_ref, comb_ref, wg_ref, wu_ref, wd_ref, g2_ref, b2_ref, o_ref,
                xs_sc, ys_sc, col_sc, seg_sm):
    e = pl.program_id(1)
    tm, d = h_ref.shape
    ne = comb_ref.shape[1]
    blk = MOE_SORT_BLOCK
    chunk = MOE_CHUNK
    ns = xs_sc.shape[0] - chunk

    @pl.when(e == 0)
    def _dispatch():
        comb = comb_ref[...]
        sel = comb != 0.0
        self32 = jnp.where(sel, 1.0, 0.0)
        selb = self32.astype(BF16)
        lane = lax.broadcasted_iota(jnp.int32, (tm, ne), 1)
        rr = lax.broadcasted_iota(jnp.int32, (blk, blk), 0)
        cc = lax.broadcasted_iota(jnp.int32, (blk, blk), 1)
        before = jnp.where(cc < rr, 1.0, 0.0).astype(BF16)
        ranks = []
        seen = jnp.zeros((1, ne), F32)
        for r0 in range(0, tm, blk):
            ranks.append(jnp.dot(before, selb[r0:r0 + blk], preferred_element_type=F32) + seen)
            seen = seen + jnp.sum(self32[r0:r0 + blk], axis=0, keepdims=True)
        rank = jnp.concatenate(ranks, axis=0)
        counts = jnp.sum(self32, axis=0, keepdims=True).astype(jnp.int32)
        lane1 = lax.broadcasted_iota(jnp.int32, (1, ne), 1)
        off = jnp.int32(0)
        offv = jnp.zeros((1, ne), F32)
        for ex in range(ne):
            n = counts[0, ex]
            seg_sm[0, ex] = off
            seg_sm[1, ex] = n
            offv = jnp.where(lane1 == ex, off.astype(F32), offv)
            off = off + ((n + (MOE_ALIGN - 1)) // MOE_ALIGN) * MOE_ALIGN
        pos = offv + rank

        def pick(idx, arr):
            return jnp.sum(jnp.where(lane == idx, arr, 0.0), axis=1, keepdims=True)

        i_lo = jnp.min(jnp.where(sel, lane, ne), axis=1, keepdims=True)
        i_hi = jnp.max(jnp.where(sel, lane, -1), axis=1, keepdims=True)
        any_sel = i_hi >= 0
        two = i_hi != i_lo
        p_lo = jnp.where(any_sel, pick(i_lo, pos), -1.0)
        p_hi = jnp.where(two, pick(i_hi, pos), -1.0)
        g_lo = pick(i_lo, comb)
        g_hi = jnp.where(two, pick(i_hi, comb), 0.0)
        l128 = lax.broadcasted_iota(jnp.int32, (tm, 128), 1)
        cols = jnp.where(l128 == 0, p_lo, jnp.where(l128 == 1, p_hi, jnp.where(l128 == 2, g_lo,
                         jnp.where(l128 == 3, g_hi, 0.0))))
        col_sc[...] = cols
        rows_t = cols.T
        p_lo_r, p_hi_r = rows_t[0:1, :], rows_t[1:2, :]
        x = h_ref[...].astype(BF16)
        for r0 in range(0, ns, blk):
            ri = (lax.broadcasted_iota(jnp.int32, (blk, tm), 0) + r0).astype(F32)
            onehot = jnp.where((ri == p_lo_r) | (ri == p_hi_r), 1.0, 0.0).astype(BF16)
            xs_sc[r0:r0 + blk, :] = jnp.dot(onehot, x, preferred_element_type=F32).astype(BF16)
        xs_sc[ns:, :] = jnp.zeros((chunk, d), BF16)
        ys_sc[...] = jnp.zeros_like(ys_sc)

    for sub in range(wg_ref.shape[1]):
        ex = e * wg_ref.shape[1] + sub
        seg_off = seg_sm[0, ex]
        seg_n = seg_sm[1, ex]

        def ffn_chunk(c, carry, sub=sub, seg_off=seg_off):
            start = pl.multiple_of(seg_off + c * chunk, MOE_ALIGN)
            x = xs_sc[pl.ds(start, chunk), :]
            gate = jnp.dot(x, wg_ref[0, sub], preferred_element_type=F32)
            up = jnp.dot(x, wu_ref[0, sub], preferred_element_type=F32)
            hid = gate * _sigmoid(gate) * up
            ys_sc[pl.ds(start, chunk), :] = jnp.dot(hid.astype(BF16), wd_ref[0, sub],
                                                    preferred_element_type=F32).astype(BF16)
            return carry

        lax.fori_loop(0, (seg_n + (chunk - 1)) // chunk, ffn_chunk, 0)

    @pl.when(e == pl.num_programs(1) - 1)
    def _combine():
        ys = ys_sc[0:ns, :]
        for r0 in range(0, tm, blk):
            cols = col_sc[r0:r0 + blk, :]
            p_lo, p_hi, g_lo, g_hi = cols[:, 0:1], cols[:, 1:2], cols[:, 2:3], cols[:, 3:4]
            li = lax.broadcasted_iota(jnp.int32, (blk, ns), 1).astype(F32)
            wsel = jnp.where(li == p_lo, g_lo, jnp.where(li == p_hi, g_hi, 0.0))
            ff = jnp.dot(wsel.astype(BF16), ys, preferred_element_type=F32)
            o_ref[r0:r0 + blk, :] = _layer_norm(ALPHA * h_ref[r0:r0 + blk, :] + ff, g2_ref[...], b2_ref[...])


def _moe(h, comb, wg, wu, wd, li, ln_g, ln_b):
    t, d = h.shape
    _, ne, _, f = wg.shape
    rows = MOE_ROW_TILE
    eps = MOE_EXPERTS_PER_STEP
    ns = -(-(2 * rows + ne * (MOE_ALIGN - 1)) // MOE_SORT_BLOCK) * MOE_SORT_BLOCK
    return pl.pallas_call(
        _moe_kernel,
        out_shape=jax.ShapeDtypeStruct((t, d), F32),
        grid=(t // rows, ne // eps),
        in_specs=[pl.BlockSpec((rows, d), lambda i, e: (i, 0)),
                  pl.BlockSpec((rows, ne), lambda i, e: (i, 0)),
                  pl.BlockSpec((1, eps, d, f), lambda i, e: (li, e, 0, 0)),
                  pl.BlockSpec((1, eps, d, f), lambda i, e: (li, e, 0, 0)),
                  pl.BlockSpec((1, eps, f, d), lambda i, e: (li, e, 0, 0)),
                  pl.BlockSpec((1, d), lambda i, e: (0, 0)),
                  pl.BlockSpec((1, d), lambda i, e: (0, 0))],
        out_specs=pl.BlockSpec((rows, d), lambda i, e: (i, 0)),
        scratch_shapes=[pltpu.VMEM((ns + MOE_CHUNK, d), BF16),
                        pltpu.VMEM((ns + MOE_CHUNK, d), BF16),
                        pltpu.VMEM((rows, 128), F32),
                        pltpu.SMEM((2, ne), jnp.int32)],
        compiler_params=pltpu.CompilerParams(dimension_semantics=("parallel", "arbitrary"),
                                             vmem_limit_bytes=MOE_VMEM_LIMIT),
        name="moe_ln2",
    )(h, comb, wg, wu, wd, ln_g, ln_b)


def _layer(h, lp, shared, batch):
    t, d = h.shape
    l = t // batch
    proj = _in_proj(h.reshape(batch, l, d), lp["w_qkv_rw"], lp["mu"], lp["w0"], lp["w_up"], lp["a0"],
                    lp["a_up"], lp["g_up"], lp["k_k"], lp["k_a"], shared["bd512"])
    o_sb = _sb_attention(*proj[:3]).reshape(t, MIX_WIDTH)
    o_rw = _rwkv_scan(*proj[3:], lp["r_k"], lp["lnx_g"], lp["lnx_b"]).reshape(t, MIX_WIDTH)
    h1, comb = _merge(h, o_sb, o_rw, lp["w_gates"], lp["p_sb"], lp["p_rwkv"], lp["w_out"],
                      lp["ln1_g"], lp["ln1_b"], shared["router_w"], shared["router_b"])
    return _moe(h1, comb, shared["wg"], shared["wu"], shared["wd"], lp["li"], lp["ln2_g"], lp["ln2_b"])


def kernel(x, meta, emb_ln_g, emb_ln_b, w_in, rwkv_mu, w0, w_up, a0, a_up, g_up, k_k, k_a, r_k, lnx_g, lnx_b, p_sb, p_rwkv, w_out, ln1_g, ln1_b, router_w, router_b, exp_w_gate, exp_w_up, exp_w_down, ln2_g, ln2_b):
    batch, seq, d = x.shape
    n_meta = meta.shape[0]
    l = seq + n_meta
    lpad = -(-l // SEQ_ALIGN) * SEQ_ALIGN
    depth = w_in.shape[0]

    h = _embed_ln(x, meta.astype(x.dtype), emb_ln_g.reshape(1, d), emb_ln_b.reshape(1, d), lpad)
    h = h.reshape(batch * lpad, d)

    idx = jnp.arange(MIX_WIDTH) // HEAD_DIM
    shared = {
        "bd512": (idx[:, None] == idx[None, :]).astype(BF16),
        "router_w": router_w.T,
        "wg": exp_w_gate.astype(BF16), "wu": exp_w_up.astype(BF16), "wd": exp_w_down.astype(BF16),
        "router_b": router_b.reshape(-1, 1),
    }
    qkv_w = 3 * MIX_WIDTH
    for li in range(depth):
        row = lambda a: a[li].reshape(1, -1)
        lp = {
            "w_qkv_rw": w_in[li, :, :qkv_w + RW_WIDTH].astype(BF16),
            "w_gates": w_in[li, :, qkv_w + RW_WIDTH:].astype(BF16),
            "mu": row(rwkv_mu), "w0": row(w0), "a0": row(a0), "k_k": row(k_k), "k_a": row(k_a),
            "w_up": w_up[li], "a_up": a_up[li], "g_up": g_up[li],
            "r_k": row(r_k), "lnx_g": row(lnx_g), "lnx_b": row(lnx_b),
            "p_sb": p_sb[li].astype(BF16), "p_rwkv": p_rwkv[li].astype(BF16),
            "w_out": w_out[li].astype(BF16),
            "ln1_g": row(ln1_g), "ln1_b": row(ln1_b), "ln2_g": row(ln2_g), "ln2_b": row(ln2_b),
            "li": li,
        }
        h = _layer(h, lp, shared, batch)
    return h.reshape(batch, lpad, d)[:, n_meta:l]
```

```python
import functools
import math

import jax
import jax.numpy as jnp
from jax import lax
from jax.experimental import pallas as pl
from jax.experimental.pallas import tpu as pltpu

F32 = jnp.float32
BF16 = jnp.bfloat16

D_MODEL = 1024
DEPTH = 2
N_META = 16
HEAD_DIM = 64
PAIR = 2 * HEAD_DIM
LANE_TILE = 128
MIX_WIDTH = 512
W_LORA, A_LORA, G_LORA = 64, 64, 128
RW_WIDTH = 3 * MIX_WIDTH + W_LORA + A_LORA + G_LORA
DECAY_SCALE = math.exp(-0.5)
LNX_EPS = 64e-5
LN_EPS = 1e-5
N_EXPERTS = 16
N_GROUPS = 4
GROUP = N_EXPERTS // N_GROUPS
D_EXPERT = 512
ALPHA = (2 * DEPTH) ** 0.25

SEQ_ALIGN = 640
ROW_TILE = 640
IN_PROJ_SPLIT = 2
ATT_BLOCK = 128
ATT_SUB = 32
ATT_PAIRS = 4
SCAN_CHUNK = 64
SCAN_CHUNKS_PER_STEP = 2
EXP_ZERO_BELOW = -104.0
MOE_ROW_TILE = 1280
MOE_CHUNK = 192
MOE_SORT_BLOCK = 256
MOE_EXPERTS_PER_STEP = 2
MOE_ALIGN = 16
VMEM_LIMIT = 48 * 1024 * 1024
MOE_VMEM_LIMIT = 58 * 1024 * 1024


def _cparams(sem):
    return pltpu.CompilerParams(dimension_semantics=sem, vmem_limit_bytes=VMEM_LIMIT)


def _split2(x):
    hi = x.astype(BF16)
    lo = (x - hi.astype(F32)).astype(BF16)
    return hi, lo


_NN = (((1,), (0,)), ((), ()))
_NT = (((1,), (1,)), ((), ()))


def _dg(a, b, dims):
    return lax.dot_general(a, b, dims, preferred_element_type=F32)


def _mm(a, b, dims=_NN):
    ah, al = _split2(a)
    bh, bl = _split2(b)
    return _dg(ah, bh, dims) + (_dg(ah, bl, dims) + _dg(al, bh, dims))


def _mm_exact_lhs(a_bf16, b, dims=_NN):
    b1, b2 = _split2(b)
    return _dg(a_bf16, b1, dims) + _dg(a_bf16, b2, dims)


def _mm_exact_rhs(a, b_bf16, dims=_NN):
    a1, a2 = _split2(a)
    return _dg(a1, b_bf16, dims) + _dg(a2, b_bf16, dims)


def _sigmoid(x):
    return 1.0 / (1.0 + jnp.exp(-x))


def _layer_norm(x, g, b):
    mu = jnp.mean(x, axis=-1, keepdims=True)
    xc = x - mu
    var = jnp.mean(xc * xc, axis=-1, keepdims=True)
    return xc * lax.rsqrt(var + LN_EPS) * g + b


def _embed_ln_kernel(x_ref, tail_ref, meta_ref, g_ref, b_ref, o_ref, *, n_real):
    i = pl.program_id(1)
    rows, n_meta = o_ref.shape[1], meta_ref.shape[0]
    head = jnp.where(i == 0, meta_ref[...], tail_ref[0])
    tile = jnp.concatenate([head, x_ref[0, :rows - n_meta, :]], axis=0)
    pos = i * rows + lax.broadcasted_iota(jnp.int32, (rows, 1), 0)
    tile = jnp.where(pos < n_real, tile, 0.0)
    o_ref[0] = _layer_norm(tile, g_ref[...], b_ref[...])


def _embed_ln(x, meta, g, b, lpad):
    bsz, seq, d = x.shape
    n_meta = meta.shape[0]
    rows = ROW_TILE
    per = rows // n_meta
    return pl.pallas_call(
        functools.partial(_embed_ln_kernel, n_real=seq + n_meta),
        out_shape=jax.ShapeDtypeStruct((bsz, lpad, d), F32),
        grid=(bsz, lpad // rows),
        in_specs=[pl.BlockSpec((1, rows, d), lambda bi, i: (bi, i, 0)),
                  pl.BlockSpec((1, n_meta, d), lambda bi, i: (bi, jnp.maximum(i * per - 1, 0), 0)),
                  pl.BlockSpec((n_meta, d), lambda bi, i: (0, 0)),
                  pl.BlockSpec((1, d), lambda bi, i: (0, 0)),
                  pl.BlockSpec((1, d), lambda bi, i: (0, 0))],
        out_specs=pl.BlockSpec((1, rows, d), lambda bi, i: (bi, i, 0)),
        compiler_params=_cparams(("parallel", "parallel")),
        name="embed_ln",
    )(x, x, meta, g, b)


def _sb_attn_kernel(q_ref, k_ref, v_ref, o_ref, acc_sc, c_sc):
    blk = ATT_BLOCK
    sub = ATT_SUB
    npair = q_ref.shape[2] // PAIR
    i = pl.program_id(2)
    first = lax.broadcasted_iota(jnp.int32, (1, PAIR), 1) < HEAD_DIM
    heads = []
    for p in range(npair):
        q = q_ref[0, :, p * PAIR:(p + 1) * PAIR]
        zero = jnp.zeros_like(q)
        heads += [(p, jnp.where(first, q, zero)), (p, jnp.where(first, zero, q))]
    rr = lax.broadcasted_iota(jnp.int32, (blk, blk), 0)
    cc = lax.broadcasted_iota(jnp.int32, (blk, blk), 1)
    causal = cc < rr
    later = (rr > cc).astype(BF16)

    def sweep(j, r0, nrows, diagonal):
        rows = slice(r0, r0 + nrows)
        start = pl.multiple_of(j * blk, blk)
        kbs = [k_ref[0, pl.ds(start, blk), p * PAIR:(p + 1) * PAIR] for p in range(npair)]
        vbs = [v_ref[0, pl.ds(start, blk), p * PAIR:(p + 1) * PAIR] for p in range(npair)]
        zs = [_dg(qh[rows], kbs[p], _NT) for p, qh in heads]
        sps = [jnp.maximum(z, 0.0) + jnp.log(1.0 + jnp.exp(-jnp.abs(z))) for z in zs]
        if diagonal:
            us = [jnp.where(causal[rows], sp, 0.0) for sp in sps]
            cs = [jnp.zeros((nrows, 1), F32)] * len(heads)
        else:
            us = sps
            cs = [c_sc[h, rows, :] for h in range(len(heads))]
        rests = [c - _mm_exact_rhs(u, later) for u, c in zip(us, cs)]
        ws = [jnp.exp((z - sp) + rest) for z, sp, rest in zip(zs, sps, rests)]
        if diagonal:
            ws = [jnp.where(causal[rows], a, 0.0) for a in ws]
        pvs = [jnp.dot(a.astype(BF16), vbs[p], preferred_element_type=F32) for (p, _), a in zip(heads, ws)]
        for p in range(npair):
            pv = jnp.where(first, pvs[2 * p], pvs[2 * p + 1])
            if diagonal:
                acc_sc[rows, p * PAIR:(p + 1) * PAIR] = pv
            else:
                acc_sc[rows, p * PAIR:(p + 1) * PAIR] += pv
        cs = [c - jnp.sum(u, axis=1, keepdims=True) for c, u in zip(cs, us)]
        for h, c in enumerate(cs):
            c_sc[h, rows, :] = c
        return cs

    def live(cs, r0, nrows):
        m = cs[0][r0:r0 + nrows]
        for c in cs[1:]:
            m = jnp.maximum(m, c[r0:r0 + nrows])
        return (jnp.max(m) > EXP_ZERO_BELOW).astype(jnp.int32)

    nsub = blk // sub
    cs = sweep(i, 0, blk, True)
    alive0 = tuple(live(cs, g * sub, sub) for g in range(nsub))

    def cond(st):
        j, alive = st
        any_alive = alive[0]
        for a in alive[1:]:
            any_alive = jnp.maximum(any_alive, a)
        return jnp.logical_and(j >= 0, any_alive > 0)

    def body(st):
        j, alive = st
        all_alive = alive[0]
        for a in alive[1:]:
            all_alive = jnp.minimum(all_alive, a)

        def whole():
            cs = sweep(j, 0, blk, False)
            return tuple(live(cs, g * sub, sub) for g in range(nsub))

        def by_group():
            return tuple(lax.cond(alive[g] > 0,
                                  lambda g=g: live(sweep(j, g * sub, sub, False), 0, sub),
                                  lambda: jnp.int32(0)) for g in range(nsub))

        return j - 1, lax.cond(all_alive > 0, whole, by_group)

    lax.while_loop(cond, body, (i - 1, alive0))
    o_ref[0] = acc_sc[...].astype(o_ref.dtype)


def _sb_attention(q, k, v):
    b, l, w = q.shape
    wb = ATT_PAIRS * PAIR
    return pl.pallas_call(
        _sb_attn_kernel,
        out_shape=jax.ShapeDtypeStruct((b, l, w), BF16),
        grid=(b, w // wb, l // ATT_BLOCK),
        in_specs=[pl.BlockSpec((1, ATT_BLOCK, wb), lambda bi, p, i: (bi, i, p)),
                  pl.BlockSpec((1, l, wb), lambda bi, p, i: (bi, 0, p)),
                  pl.BlockSpec((1, l, wb), lambda bi, p, i: (bi, 0, p))],
        out_specs=pl.BlockSpec((1, ATT_BLOCK, wb), lambda bi, p, i: (bi, i, p)),
        scratch_shapes=[pltpu.VMEM((ATT_BLOCK, wb), F32),
                        pltpu.VMEM((wb // HEAD_DIM, ATT_BLOCK, 1), F32)],
        compiler_params=_cparams(("parallel", "parallel", "arbitrary")),
        name="sb_attention",
    )(q, k, v)


def _in_proj_kernel(h_ref, w_ref, mu_ref, w0_ref, wup_ref, a0_ref, aup_ref, gup_ref,
                    kk_ref, ka_ref, bd_ref,
                    q_o, ka_o, va_o, r_o, lw_o, k_o, v_o, kn_o, b_o, g_o, prev_sc):
    t = pl.program_id(1)

    @pl.when(t == 0)
    def _():
        prev_sc[...] = jnp.zeros_like(prev_sc)

    def lora(x, up_ref):
        return jnp.dot(x.astype(BF16), up_ref[...].astype(BF16), preferred_element_type=F32)

    w = MIX_WIDTH
    rows = h_ref.shape[1] // IN_PROJ_SPLIT
    row = lax.broadcasted_iota(jnp.int32, (rows, 1), 0)
    prev = prev_sc[...]
    for part in range(IN_PROJ_SPLIT):
        rs = slice(part * rows, (part + 1) * rows)
        x = h_ref[0, rs, :].astype(BF16)
        pall = jnp.dot(x, w_ref[...], preferred_element_type=F32)
        q_o[0, rs, :] = (pall[:, :w] * (HEAD_DIM ** -0.5)).astype(BF16)
        ka_o[0, rs, :] = pall[:, w:2 * w].astype(BF16)
        va_o[0, rs, :] = pall[:, 2 * w:3 * w].astype(BF16)
        p = pall[:, 3 * w:]
        shifted = jnp.where(row == 0, prev, pltpu.roll(p, 1, axis=0))
        prev = p[rows - 1:rows, :]
        xs = p + (shifted - p) * mu_ref[...]

        r, k, v = xs[:, :w], xs[:, w:2 * w], xs[:, 2 * w:3 * w]
        wd = xs[:, 3 * w:3 * w + W_LORA]
        ad = xs[:, 3 * w + W_LORA:3 * w + W_LORA + A_LORA]
        gd = xs[:, 3 * w + W_LORA + A_LORA:]

        lw = -DECAY_SCALE * _sigmoid(w0_ref[...] + lora(jnp.tanh(wd), wup_ref))
        a = _sigmoid(a0_ref[...] + lora(ad, aup_ref))
        g = lora(_sigmoid(gd), gup_ref)

        kk = k * kk_ref[...]
        ssq = jnp.dot((kk * kk).astype(BF16), bd_ref[...], preferred_element_type=F32)
        kn = kk / jnp.maximum(jnp.sqrt(ssq), 1e-12)
        r_o[0, rs, :] = r
        lw_o[0, rs, :] = lw
        k_o[0, rs, :] = k * (1.0 + (a - 1.0) * ka_ref[...])
        v_o[0, rs, :] = v
        kn_o[0, rs, :] = kn
        b_o[0, rs, :] = kn * a
        g_o[0, rs, :] = g
    prev_sc[...] = prev


def _in_proj(h3, w_qkv_rw, mu, w0, w_up, a0, a_up, g_up, k_k, k_a, bd512):
    b, l, d = h3.shape
    nt = l // ROW_TILE
    out = jax.ShapeDtypeStruct((b, l, MIX_WIDTH), F32)
    out16 = jax.ShapeDtypeStruct((b, l, MIX_WIDTH), BF16)
    row = pl.BlockSpec((1, ROW_TILE, MIX_WIDTH), lambda bi, t: (bi, t, 0))

    def full(arr):
        return pl.BlockSpec(arr.shape, lambda bi, t: (0,) * arr.ndim)

    consts = (w_qkv_rw, mu, w0, w_up, a0, a_up, g_up, k_k, k_a, bd512)
    return pl.pallas_call(
        _in_proj_kernel,
        out_shape=(out16,) * 3 + (out,) * 7,
        grid=(b, nt),
        in_specs=[pl.BlockSpec((1, ROW_TILE, d), lambda bi, t: (bi, t, 0))] + [full(c) for c in consts],
        out_specs=(row,) * 10,
        scratch_shapes=[pltpu.VMEM((1, RW_WIDTH), F32)],
        compiler_params=pltpu.CompilerParams(dimension_semantics=("parallel", "arbitrary"),
                                             vmem_limit_bytes=MOE_VMEM_LIMIT),
        name="in_proj",
    )(h3, *consts)


def _rwkv_scan_kernel(r_ref, lw_ref, k_ref, v_ref, kn_ref, b_ref, g_ref,
                      rk_ref, lng_ref, lnb_ref, o_ref, state_sc):
    nb, rows, w = r_ref.shape
    c = SCAN_CHUNK
    nchunk = rows // c
    npair = w // PAIR
    assert 2 * c == PAIR

    @pl.when(pl.program_id(0) == 0)
    def _():
        state_sc[...] = jnp.zeros_like(state_sc)

    rr = lax.broadcasted_iota(jnp.int32, (c, c), 0)
    cc = lax.broadcasted_iota(jnp.int32, (c, c), 1)
    ones_incl = (rr >= cc).astype(BF16)
    row_c = lax.broadcasted_iota(jnp.int32, (c, PAIR), 0)
    key_c = lax.broadcasted_iota(jnp.int32, (c, PAIR), 1) & (HEAD_DIM - 1)
    strict_m = key_c < row_c
    incl_m = key_c <= row_c
    first = lax.broadcasted_iota(jnp.int32, (1, PAIR), 1) < HEAD_DIM
    pr = lax.broadcasted_iota(jnp.int32, (PAIR, PAIR), 0)
    pc = lax.broadcasted_iota(jnp.int32, (PAIR, PAIR), 1)
    same_head = (pr < HEAD_DIM) == (pc < HEAD_DIM)
    head0_rows = lax.broadcasted_iota(jnp.int32, (PAIR, 2 * PAIR), 0) < HEAD_DIM
    head_ones = same_head.astype(BF16)
    eye_p = pr == pc
    eye_sbs = (key_c == row_c).astype(F32)
    zeros_c = jnp.zeros((c, PAIR), F32)

    def cat0(*xs):
        return jnp.concatenate(xs, axis=0)

    def cat1(*xs):
        return jnp.concatenate(xs, axis=1)

    def mm(a, b, dims=_NN):
        return _dg(a.astype(BF16), b.astype(BF16), dims)

    streams = [(bi, p, ci) for bi in range(nb) for p in range(npair) for ci in range(nchunk)]

    def sl_of(s):
        bi, p, ci = s
        return (bi, slice(ci * c, (ci + 1) * c), slice(p * PAIR, (p + 1) * PAIR))

    pre = {}
    for bi in range(nb):
        for ci in range(nchunk):
            rs = (bi, slice(ci * c, (ci + 1) * c), slice(None))
            lw = lw_ref[rs]
            cl = _mm_exact_lhs(ones_incl, lw)
            cl_end = cl[c - 1:c, :]
            e_pos = jnp.exp(cl)
            e_neg = jnp.exp(-cl)
            e_end = jnp.exp(cl_end - cl)
            r, k, kn, b = r_ref[rs], k_ref[rs], kn_ref[rs], b_ref[rs]
            full = dict(a_t=-kn * jnp.exp(cl - lw), r_t=r * e_pos, b_t=b * e_neg, k_t=k * e_neg,
                        b_h=b * e_end, k_h=k * e_end, g_end=jnp.exp(cl_end))
            for p in range(npair):
                pre[(bi, p, ci)] = {n: x[:, p * PAIR:(p + 1) * PAIR] for n, x in full.items()}

    tri = {}
    for s in streams:
        d = pre[s]
        a_t, r_t, b_t, k_t = d["a_t"], d["r_t"], d["b_t"], d["k_t"]
        bk = cat0(b_t, k_t)
        s0 = mm(jnp.where(first, cat0(a_t, r_t), 0.0), bk, _NT)
        s1t = mm(jnp.where(first, 0.0, a_t), cat0(k_t, b_t), _NT)
        s1b = mm(jnp.where(first, 0.0, r_t), bk, _NT)
        top0 = jnp.where(strict_m, s0[:c], 0.0)
        top1 = jnp.where(strict_m, s1t, 0.0)
        tri[s] = dict(
            l_ab_sbs=jnp.where(first, top0, top1),
            l_ak=cat0(jnp.where(first, 0.0, top0), jnp.where(first, top1, 0.0)),
            m0=jnp.where(incl_m, s0[c:], 0.0), m1=jnp.where(incl_m, s1b, 0.0))

    xs = {}
    for s in streams:
        v = v_ref[sl_of(s)]
        xs[s] = mm(tri[s]["l_ak"], cat0(v, v))

    def diag2(x):
        return cat0(jnp.where(first, x, 0.0), jnp.where(first, 0.0, x))

    pw = {s: tri[s]["l_ab_sbs"] for s in streams}
    tinv = {s: eye_sbs + pw[s] for s in streams}
    sq = {s: mm(pw[s], diag2(pw[s])) for s in streams}
    n = 2
    while 2 * n < c:
        for s in streams:
            both = mm(cat0(sq[s], tinv[s]), diag2(sq[s]))
            tinv[s] = tinv[s] + both[c:]
            sq[s] = both[:c]
        n *= 2
    for s in streams:
        tinv[s] = diag2(tinv[s] + mm(tinv[s], diag2(sq[s])))

    pq = {}
    for s in streams:
        a_t = pre[s]["a_t"]
        rhs = cat0(cat1(a_t, xs[s][:c]), cat1(a_t, xs[s][c:]))
        pq[s] = mm(tinv[s], rhs)

    coef = {}
    for s in streams:
        d = pre[s]
        v = v_ref[sl_of(s)]
        bkh_t = cat0(d["b_h"], d["k_h"]).T
        zv = cat1(zeros_c, v)
        res0 = mm(cat0(tri[s]["m0"], bkh_t), cat0(pq[s][:c], zv))
        res1 = mm(cat0(tri[s]["m1"], bkh_t), cat0(pq[s][c:], zv))
        ry = d["r_t"] + jnp.where(first, res0[:c, :PAIR], res1[:c, :PAIR])
        y0 = jnp.where(first, res0[:c, PAIR:], res1[:c, PAIR:])
        gj = jnp.where(head0_rows, res0[c:], res1[c:])
        gmat = jnp.where(same_head, gj[:, :PAIR], 0.0) + jnp.where(eye_p, d["g_end"], 0.0)
        jmat = jnp.where(same_head, gj[:, PAIR:], 0.0)
        coef[s] = (cat0(ry, gmat), y0, jmat)

    states = [state_sc[idx] for idx in range(nb * npair)]
    ys = {}
    for bi in range(nb):
        for p in range(npair):
            idx = bi * npair + p
            for ci in range(nchunk):
                lhs, y0, jmat = coef[(bi, p, ci)]
                st = mm(lhs, states[idx])
                ys[(bi, p, ci)] = st[:c] + y0
                states[idx] = st[c:] + jmat
    for idx, state in enumerate(states):
        state_sc[idx] = state

    sums = {}
    for s in streams:
        sl = sl_of(s)
        sums[s] = mm(cat0(ys[s], r_ref[sl] * k_ref[sl] * rk_ref[:, sl[2]]), head_ones)
    yc = {s: ys[s] - sums[s][:c] * (1.0 / HEAD_DIM) for s in streams}
    var = {s: mm(yc[s] * yc[s], head_ones) * (1.0 / HEAD_DIM) for s in streams}
    for s in streams:
        sl = sl_of(s)
        csl = (slice(None), sl[2])
        yn = yc[s] * lax.rsqrt(var[s] + LNX_EPS) * lng_ref[csl] + lnb_ref[csl]
        bonus = sums[s][c:] * v_ref[sl]
        o_ref[sl] = ((yn + bonus) * g_ref[sl]).astype(o_ref.dtype)


def _rwkv_scan(r, lw, k, v, kn, bb, g, r_k, lnx_g, lnx_b):
    b, l, w = r.shape
    c = SCAN_CHUNK * SCAN_CHUNKS_PER_STEP
    row = pl.BlockSpec((b, c, w), lambda i: (0, i, 0))
    vec = pl.BlockSpec((1, w), lambda i: (0, 0))
    return pl.pallas_call(
        _rwkv_scan_kernel,
        out_shape=jax.ShapeDtypeStruct((b, l, w), BF16),
        grid=(l // c,),
        in_specs=[row] * 7 + [vec] * 3,
        out_specs=row,
        scratch_shapes=[pltpu.VMEM((b * (w // PAIR), PAIR, PAIR), F32)],
        compiler_params=_cparams(("arbitrary",)),
        name="rwkv_scan",
    )(r, lw, k, v, kn, bb, g, r_k, lnx_g, lnx_b)


def _merge_kernel(h_ref, osb_ref, orw_ref, wg_ref, psb_ref, prw_ref, wout_ref,
                  g1_ref, b1_ref, rw_ref, rb_ref, h1_ref, comb_ref):
    h = h_ref[...]
    hb = h.astype(BF16)
    gates = _sigmoid(jnp.dot(hb, wg_ref[...], preferred_element_type=F32))
    sb = jnp.dot(osb_ref[...], psb_ref[...], preferred_element_type=F32)
    rw = jnp.dot(orw_ref[...], prw_ref[...], preferred_element_type=F32)
    merged = gates[:, :D_MODEL] * sb + gates[:, D_MODEL:] * rw
    mix = jnp.dot(merged.astype(BF16), wout_ref[...], preferred_element_type=F32)
    h1 = _layer_norm(ALPHA * h + mix, g1_ref[...], b1_ref[...])
    h1_ref[...] = h1

    logits = _mm(rw_ref[...], h1, _NT) + rb_ref[...]
    m = jnp.max(logits, axis=0, keepdims=True)
    ex = jnp.exp(logits - m)
    probs = ex / jnp.sum(ex, axis=0, keepdims=True)
    eid = lax.broadcasted_iota(jnp.int32, probs.shape, 0)
    big = N_EXPERTS

    def top2(vals):
        v1 = jnp.max(vals, axis=0, keepdims=True)
        i1 = jnp.min(jnp.where(vals == v1, eid, big), axis=0, keepdims=True)
        rest = jnp.where(eid == i1, -2.0, vals)
        v2 = jnp.max(rest, axis=0, keepdims=True)
        i2 = jnp.min(jnp.where(rest == v2, eid, big), axis=0, keepdims=True)
        return v1, i1, v2, i2

    best = None
    sel = None
    for gi in range(N_GROUPS):
        in_g = (eid >= gi * GROUP) & (eid < (gi + 1) * GROUP)
        v1, _, v2, _ = top2(jnp.where(in_g, probs, -1.0))
        score = v1 + v2
        if gi == 0:
            best, sel = score, jnp.zeros_like(score, dtype=jnp.int32)
        else:
            better = score > best
            best = jnp.where(better, score, best)
            sel = jnp.where(better, gi, sel)
    in_sel = (eid >= sel * GROUP) & (eid < (sel + 1) * GROUP)
    v1, i1, v2, i2 = top2(jnp.where(in_sel, probs, -1.0))
    denom = v1 + v2
    comb_t = jnp.where(eid == i1, v1 / denom, 0.0) + jnp.where(eid == i2, v2 / denom, 0.0)
    pad = jnp.zeros((LANE_TILE - N_EXPERTS, comb_t.shape[1]), F32)
    comb_ref[...] = jnp.concatenate([comb_t, pad], axis=0).T[:, :N_EXPERTS]


def _merge(h, o_sb, o_rw, w_gates, p_sb, p_rwkv, w_out, ln_g, ln_b, router_w, router_b):
    t, d = h.shape

    def full(arr):
        return pl.BlockSpec(arr.shape, lambda i: (0,) * arr.ndim)

    consts = (w_gates, p_sb, p_rwkv, w_out, ln_g, ln_b, router_w, router_b)
    return pl.pallas_call(
        _merge_kernel,
        out_shape=(jax.ShapeDtypeStruct((t, d), F32), jax.ShapeDtypeStruct((t, N_EXPERTS), F32)),
        grid=(t // ROW_TILE,),
        in_specs=[pl.BlockSpec((ROW_TILE, d), lambda i: (i, 0)),
                  pl.BlockSpec((ROW_TILE, MIX_WIDTH), lambda i: (i, 0)),
                  pl.BlockSpec((ROW_TILE, MIX_WIDTH), lambda i: (i, 0))] + [full(c) for c in consts],
        out_specs=(pl.BlockSpec((ROW_TILE, d), lambda i: (i, 0)),
                   pl.BlockSpec((ROW_TILE, N_EXPERTS), lambda i: (i, 0))),
        compiler_params=_cparams(("parallel",)),
        name="merge_ln1_router",
    )(h, o_sb, o_rw, *consts)


def _moe_kernel(h_ref, comb_ref, wg_ref, wu_ref, wd_ref, g2_ref, b2_ref, o_ref,
                xs_sc, ys_sc, col_sc, seg_sm):
    e = pl.program_id(1)
    tm, d = h_ref.shape
    ne = comb_ref.shape[1]
    blk = MOE_SORT_BLOCK
    chunk = MOE_CHUNK
    ns = xs_sc.shape[0] - chunk

    @pl.when(e == 0)
    def _dispatch():
        comb = comb_ref[...]
        sel = comb != 0.0
        self32 = jnp.where(sel, 1.0, 0.0)
        selb = self32.astype(BF16)
        lane = lax.broadcasted_iota(jnp.int32, (tm, ne), 1)
        rr = lax.broadcasted_iota(jnp.int32, (blk, blk), 0)
        cc = lax.broadcasted_iota(jnp.int32, (blk, blk), 1)
        before = jnp.where(cc < rr, 1.0, 0.0).astype(BF16)
        ranks = []
        seen = jnp.zeros((1, ne), F32)
        for r0 in range(0, tm, blk):
            ranks.append(jnp.dot(before, selb[r0:r0 + blk], preferred_element_type=F32) + seen)
            seen = seen + jnp.sum(self32[r0:r0 + blk], axis=0, keepdims=True)
        rank = jnp.concatenate(ranks, axis=0)
        counts = jnp.sum(self32, axis=0, keepdims=True).astype(jnp.int32)
        lane1 = lax.broadcasted_iota(jnp.int32, (1, ne), 1)
        off = jnp.int32(0)
        offv = jnp.zeros((1, ne), F32)
        for ex in range(ne):
            n = counts[0, ex]
            seg_sm[0, ex] = off
            seg_sm[1, ex] = n
            offv = jnp.where(lane1 == ex, off.astype(F32), offv)
            off = off + ((n + (MOE_ALIGN - 1)) // MOE_ALIGN) * MOE_ALIGN
        pos = offv + rank

        def pick(idx, arr):
            return jnp.sum(jnp.where(lane == idx, arr, 0.0), axis=1, keepdims=True)

        i_lo = jnp.min(jnp.where(sel, lane, ne), axis=1, keepdims=True)
        i_hi = jnp.max(jnp.where(sel, lane, -1), axis=1, keepdims=True)
        any_sel = i_hi >= 0
        two = i_hi != i_lo
        p_lo = jnp.where(any_sel, pick(i_lo, pos), -1.0)
        p_hi = jnp.where(two, pick(i_hi, pos), -1.0)
        g_lo = pick(i_lo, comb)
        g_hi = jnp.where(two, pick(i_hi, comb), 0.0)
        l128 = lax.broadcasted_iota(jnp.int32, (tm, 128), 1)
        cols = jnp.where(l128 == 0, p_lo, jnp.where(l128 == 1, p_hi, jnp.where(l128 == 2, g_lo,
                         jnp.where(l128 == 3, g_hi, 0.0))))
        col_sc[...] = cols
        rows_t = cols.T
        p_lo_r, p_hi_r = rows_t[0:1, :], rows_t[1:2, :]
        x = h_ref[...].astype(BF16)
        for r0 in range(0, ns, blk):
            ri = (lax.broadcasted_iota(jnp.int32, (blk, tm), 0) + r0).astype(F32)
            onehot = jnp.where((ri == p_lo_r) | (ri == p_hi_r), 1.0, 0.0).astype(BF16)
            xs_sc[r0:r0 + blk, :] = jnp.dot(onehot, x, preferred_element_type=F32).astype(BF16)
        xs_sc[ns:, :] = jnp.zeros((chunk, d), BF16)
        ys_sc[...] = jnp.zeros_like(ys_sc)

    for sub in range(wg_ref.shape[1]):
        ex = e * wg_ref.shape[1] + sub
        seg_off = seg_sm[0, ex]
        seg_n = seg_sm[1, ex]

        def ffn_chunk(c, carry, sub=sub, seg_off=seg_off):
            start = pl.multiple_of(seg_off + c * chunk, MOE_ALIGN)
            x = xs_sc[pl.ds(start, chunk), :]
            gate = jnp.dot(x, wg_ref[0, sub], preferred_element_type=F32)
            up = jnp.dot(x, wu_ref[0, sub], preferred_element_type=F32)
            hid = gate * _sigmoid(gate) * up
            ys_sc[pl.ds(start, chunk), :] = jnp.dot(hid.astype(BF16), wd_ref[0, sub],
                                                    preferred_element_type=F32).astype(BF16)
            return carry

        lax.fori_loop(0, (seg_n + (chunk - 1)) // chunk, ffn_chunk, 0)

    @pl.when(e == pl.num_programs(1) - 1)
    def _combine():
        ys = ys_sc[0:ns, :]
        for r0 in range(0, tm, blk):
            cols = col_sc[r0:r0 + blk, :]
            p_lo, p_hi, g_lo, g_hi = cols[:, 0:1], cols[:, 1:2], cols[:, 2:3], cols[:, 3:4]
            li = lax.broadcasted_iota(jnp.int32, (blk, ns), 1).astype(F32)
            wsel = jnp.where(li == p_lo, g_lo, jnp.where(li == p_hi, g_hi, 0.0))
            ff = jnp.dot(wsel.astype(BF16), ys, preferred_element_type=F32)
            o_ref[r0:r0 + blk, :] = _layer_norm(ALPHA * h_ref[r0:r0 + blk, :] + ff, g2_ref[...], b2_ref[...])


def _moe(h, comb, wg, wu, wd, li, ln_g, ln_b):
    t, d = h.shape
    _, ne, _, f = wg.shape
    rows = MOE_ROW_TILE
    eps = MOE_EXPERTS_PER_STEP
    ns = -(-(2 * rows + ne * (MOE_ALIGN - 1)) // MOE_SORT_BLOCK) * MOE_SORT_BLOCK
    return pl.pallas_call(
        _moe_kernel,
        out_shape=jax.ShapeDtypeStruct((t, d), F32),
        grid=(t // rows, ne // eps),
        in_specs=[pl.BlockSpec((rows, d), lambda i, e: (i, 0)),
                  pl.BlockSpec((rows, ne), lambda i, e: (i, 0)),
                  pl.BlockSpec((1, eps, d, f), lambda i, e: (li, e, 0, 0)),
                  pl.BlockSpec((1, eps, d, f), lambda i, e: (li, e, 0, 0)),
                  pl.BlockSpec((1, eps, f, d), lambda i, e: (li, e, 0, 0)),
                  pl.BlockSpec((1, d), lambda i, e: (0, 0)),
                  pl.BlockSpec((1, d), lambda i, e: (0, 0))],
        out_specs=pl.BlockSpec((rows, d), lambda i, e: (i, 0)),
        scratch_shapes=[pltpu.VMEM((ns + MOE_CHUNK, d), BF16),
                        pltpu.VMEM((ns + MOE_CHUNK, d), BF16),
                        pltpu.VMEM((rows, 128), F32),
                        pltpu.SMEM((2, ne), jnp.int32)],
        compiler_params=pltpu.CompilerParams(dimension_semantics=("parallel", "arbitrary"),
                                             vmem_limit_bytes=MOE_VMEM_LIMIT),
        name="moe_ln2",
    )(h, comb, wg, wu, wd, ln_g, ln_b)


def _layer(h, lp, shared, batch):
    t, d = h.shape
    l = t // batch
    proj = _in_proj(h.reshape(batch, l, d), lp["w_qkv_rw"], lp["mu"], lp["w0"], lp["w_up"], lp["a0"],
                    lp["a_up"], lp["g_up"], lp["k_k"], lp["k_a"], shared["bd512"])
    o_sb = _sb_attention(*proj[:3]).reshape(t, MIX_WIDTH)
    o_rw = _rwkv_scan(*proj[3:], lp["r_k"], lp["lnx_g"], lp["lnx_b"]).reshape(t, MIX_WIDTH)
    h1, comb = _merge(h, o_sb, o_rw, lp["w_gates"], lp["p_sb"], lp["p_rwkv"], lp["w_out"],
                      lp["ln1_g"], lp["ln1_b"], shared["router_w"], shared["router_b"])
    return _moe(h1, comb, shared["wg"], shared["wu"], shared["wd"], lp["li"], lp["ln2_g"], lp["ln2_b"])


def kernel(x, meta, emb_ln_g, emb_ln_b, w_in, rwkv_mu, w0, w_up, a0, a_up, g_up, k_k, k_a, r_k, lnx_g, lnx_b, p_sb, p_rwkv, w_out, ln1_g, ln1_b, router_w, router_b, exp_w_gate, exp_w_up, exp_w_down, ln2_g, ln2_b):
    batch, seq, d = x.shape
    n_meta = meta.shape[0]
    l = seq + n_meta
    lpad = -(-l // SEQ_ALIGN) * SEQ_ALIGN
    depth = w_in.shape[0]

    h = _embed_ln(x, meta.astype(x.dtype), emb_ln_g.reshape(1, d), emb_ln_b.reshape(1, d), lpad)
    h = h.reshape(batch * lpad, d)

    idx = jnp.arange(MIX_WIDTH) // HEAD_DIM
    shared = {
        "bd512": (idx[:, None] == idx[None, :]).astype(BF16),
        "router_w": router_w.T,
        "wg": exp_w_gate.astype(BF16), "wu": exp_w_up.astype(BF16), "wd": exp_w_down.astype(BF16),
        "router_b": router_b.reshape(-1, 1),
    }
    qkv_w = 3 * MIX_WIDTH
    for li in range(depth):
        row = lambda a: a[li].reshape(1, -1)
        lp = {
            "w_qkv_rw": w_in[li, :, :qkv_w + RW_WIDTH].astype(BF16),
            "w_gates": w_in[li, :, qkv_w + RW_WIDTH:].astype(BF16),
            "mu": row(rwkv_mu), "w0": row(w0), "a0": row(a0), "k_k": row(k_k), "k_a": row(k_a),
            "w_up": w_up[li], "a_up": a_up[li], "g_up": g_up[li],
            "r_k": row(r_k), "lnx_g": row(lnx_g), "lnx_b": row(lnx_b),
            "p_sb": p_sb[li].astype(BF16), "p_rwkv": p_rwkv[li].astype(BF16),
            "w_out": w_out[li].astype(BF16),
            "ln1_g": row(ln1_g), "ln1_b": row(ln1_b), "ln2_g": row(ln2_g), "ln2_b": row(ln2_b),
            "li": li,
        }
        h = _layer(h, lp, shared, batch)
    return h.reshape(batch, lpad, d)[:, n_meta:l]
```
